```python
import jax, jax.numpy as jnp
from jax import lax
import numpy as np

D_MODEL = 4096
BATCH = 1
SEQ = 8192
DEPTH = 4
DEC_BATCH = 8
DEC_SEQ = 64
PAST_LEN = 2048

CHUNK = 64
N_META = 16
QB = 128
D_A = D_MODEL // 4
CONV_K = 31
DH_B = 128
H_B = D_MODEL // 4 // DH_B
D_B = H_B * DH_B
DH_NOPE = 128
DH_ROPE = 64
DH_V = 128
H_C = D_MODEL // 2 // DH_V
D_C = H_C * DH_V
Q_RANK = 3 * D_MODEL // 8
KV_RANK = D_MODEL // 8
D_MIX = D_A + D_B + D_C
SPLITS = (D_A, D_A, D_A,
          D_B, D_B, D_B, D_B,
          Q_RANK, KV_RANK, DH_ROPE, D_C)
N_IN = sum(SPLITS)
EPS = 1e-6
ROPE_BASE = 10000.0
NEG_INF = -1e30

kernel_name = 'hymba_conformer_stickbreak_mla_stream_step'


def rmsnorm(x, g):
    xf = x.astype(jnp.float32)
    y = xf * lax.rsqrt(jnp.mean(xf * xf, axis=-1, keepdims=True) + EPS)
    return (y * g.astype(jnp.float32)).astype(x.dtype)


def layernorm(x, g, b):
    xf = x.astype(jnp.float32)
    xc = xf - jnp.mean(xf, axis=-1, keepdims=True)
    y = xc * lax.rsqrt(jnp.mean(xc * xc, axis=-1, keepdims=True) + EPS)
    return (y * g.astype(jnp.float32) + b.astype(jnp.float32)).astype(x.dtype)


def rope(x, pos):
    half = x.shape[-1] // 2
    inv_freq = ROPE_BASE ** (-jnp.arange(half, dtype=jnp.float32) / half)
    ang = pos.astype(jnp.float32)[:, None] * inv_freq[None, :]
    shape = (1, pos.shape[0]) + (1,) * (x.ndim - 3) + (half,)
    cos = jnp.cos(ang).reshape(shape)
    sin = jnp.sin(ang).reshape(shape)
    xf = x.astype(jnp.float32)
    x1, x2 = xf[..., :half], xf[..., half:]
    return jnp.concatenate([x1 * cos - x2 * sin, x1 * sin + x2 * cos], axis=-1).astype(x.dtype)


def causal_dwconv(u_full, w, b):
    out = lax.conv_general_dilated(u_full, w[:, None, :].astype(u_full.dtype), window_strides=(1,),
                                   padding='VALID', dimension_numbers=('NWC', 'WIO', 'NWC'),
                                   feature_group_count=u_full.shape[-1])
    return out + b.astype(out.dtype)


def to_blocks(a, nblk):
    pad = nblk * QB - a.shape[1]
    a = jnp.pad(a, [(0, 0), (0, pad)] + [(0, 0)] * (a.ndim - 2))
    a = a.reshape((a.shape[0], nblk, QB) + a.shape[2:])
    return jnp.moveaxis(a, 1, 0)


def pos_blocks(p, nblk):
    return jnp.pad(p, (0, nblk * QB - p.shape[0])).reshape(nblk, QB)


def from_blocks(o, T):
    o = jnp.moveaxis(o, 0, 1)
    return o.reshape((o.shape[0], -1) + o.shape[3:])[:, :T]


def sb_attention(q, k, v, q_pos, k_pos):
    Tq, dh = q.shape[1], q.shape[3]
    nblk = -(-Tq // QB)
    scale = dh ** -0.5

    def block(args):
        qi, pi = args
        z = jnp.einsum('bqhd,bkhd->bhqk', qi, k, preferred_element_type=jnp.float32) * scale
        mask = k_pos[None, :] < pi[:, None]
        log_rest = jnp.where(mask, jax.nn.log_sigmoid(-z), 0.0)
        s_excl = lax.cumsum(log_rest, axis=3, reverse=True) - log_rest
        w = jnp.where(mask, jnp.exp(jax.nn.log_sigmoid(z) + s_excl), 0.0)
        return jnp.einsum('bhqk,bkhd->bqhd', w.astype(v.dtype), v)

    o = lax.map(block, (to_blocks(q, nblk), pos_blocks(q_pos, nblk)))
    return from_blocks(o, Tq)


def mla_attention(q_lat, q_rope, lat, kr, q_chunk, k_chunk):
    Tq = q_lat.shape[1]
    nblk = -(-Tq // QB)
    scale = (DH_NOPE + DH_ROPE) ** -0.5

    def block(args):
        ql, qr, ci = args
        s = (jnp.einsum('bqhc,bkc->bhqk', ql, lat, preferred_element_type=jnp.float32)
             + jnp.einsum('bqhr,bkr->bhqk', qr, kr, preferred_element_type=jnp.float32)) * scale
        mask = k_chunk[None, :] <= ci[:, None]
        p = jax.nn.softmax(jnp.where(mask, s, NEG_INF), axis=-1)
        return jnp.einsum('bhqk,bkc->bqhc', p.astype(lat.dtype), lat)

    o = lax.map(block, (to_blocks(q_lat, nblk), to_blocks(q_rope, nblk), pos_blocks(q_chunk, nblk)))
    return from_blocks(o, Tq)


def hybrid_layer(h, conv_prev, bk_prev, bv_prev, lat_prev, kr_prev, q_pos, k_pos, q_chunk, k_chunk,
                 g_norm, w_in, dw_w, dw_b, ln_g, ln_b, cq_g, w_uq, ckv_g, w_uk, w_uv, w_out):
    Bn, T, _ = h.shape
    z = rmsnorm(h, g_norm) @ w_in
    idx = [int(i) for i in np.cumsum(SPLITS)[:-1]]
    (a_val, a_glu, a_gate, b_q, b_k, b_v, b_gate, c_q, c_kv, c_kr, c_gate) = jnp.split(z, idx, axis=-1)

    u = a_val * jax.nn.sigmoid(a_glu)
    u_full = jnp.concatenate([conv_prev, u], axis=1)
    conv_new = u_full[:, -(CONV_K - 1):]
    ya = jax.nn.silu(layernorm(causal_dwconv(u_full, dw_w, dw_b), ln_g, ln_b)) * jax.nn.silu(a_gate)

    k_new = b_k.reshape(Bn, T, H_B, DH_B)
    v_new = b_v.reshape(Bn, T, H_B, DH_B)
    k_all = jnp.concatenate([bk_prev, k_new], axis=1)
    v_all = jnp.concatenate([bv_prev, v_new], axis=1)
    yb = sb_attention(b_q.reshape(Bn, T, H_B, DH_B), k_all, v_all, q_pos, k_pos).reshape(Bn, T, D_B)
    yb = yb * jax.nn.silu(b_gate)

    qc = (rmsnorm(c_q, cq_g) @ w_uq).reshape(Bn, T, H_C, DH_NOPE + DH_ROPE)
    q_nope = qc[..., :DH_NOPE]
    q_rope = rope(qc[..., DH_NOPE:], q_pos)
    lat_new = rmsnorm(c_kv, ckv_g)
    kr_new = rope(c_kr, q_pos)
    lat_all = jnp.concatenate([lat_prev, lat_new], axis=1)
    kr_all = jnp.concatenate([kr_prev, kr_new], axis=1)
    q_lat = jnp.einsum('bthn,chn->bthc', q_nope, w_uk)
    o_lat = mla_attention(q_lat, q_rope, lat_all, kr_all, q_chunk, k_chunk)
    yc = jnp.einsum('bthc,chd->bthd', o_lat, w_uv).reshape(Bn, T, D_C) * jax.nn.silu(c_gate)

    y = jnp.concatenate([ya, yb, yc], axis=-1) @ w_out
    return h + y, conv_new, k_new, v_new, lat_new, kr_new


def setup_inputs(seed: int = 0) -> dict:
    key = jax.random.key(seed)
    ks = jax.random.split(key, 21)
    f32 = jnp.float32

    def nrm(k, shape, scale):
        return jax.random.normal(k, shape, f32) * scale

    return {
        'x_prompt': nrm(ks[0], (BATCH, SEQ, D_MODEL), 1.0),
        'x_sample': nrm(ks[1], (DEC_BATCH, DEC_SEQ, D_MODEL), 1.0),
        'cache_a_conv': nrm(ks[2], (DEPTH, DEC_BATCH, CONV_K - 1, D_A), 0.5),
        'cache_b_k': nrm(ks[3], (DEPTH, DEC_BATCH, PAST_LEN, H_B, DH_B), 1.0),
        'cache_b_v': nrm(ks[4], (DEPTH, DEC_BATCH, PAST_LEN, H_B, DH_B), 1.0),
        'cache_c_latent': nrm(ks[5], (DEPTH, DEC_BATCH, PAST_LEN, KV_RANK), 1.0),
        'cache_c_krope': nrm(ks[6], (DEPTH, DEC_BATCH, PAST_LEN, DH_ROPE), 1.0),
        'meta_tokens': nrm(ks[7], (N_META, D_MODEL), 1.0),
        'norm_g': 1.0 + nrm(ks[8], (DEPTH, D_MODEL), 0.01),
        'w_in': nrm(ks[9], (DEPTH, D_MODEL, N_IN), D_MODEL ** -0.5),
        'a_dw_w': nrm(ks[10], (DEPTH, CONV_K, D_A), CONV_K ** -0.5),
        'a_dw_b': nrm(ks[11], (DEPTH, D_A), 0.01),
        'a_ln_g': 1.0 + nrm(ks[12], (DEPTH, D_A), 0.01),
        'a_ln_b': nrm(ks[13], (DEPTH, D_A), 0.01),
        'c_q_norm_g': 1.0 + nrm(ks[14], (DEPTH, Q_RANK), 0.01),
        'c_w_uq': nrm(ks[15], (DEPTH, Q_RANK, H_C * (DH_NOPE + DH_ROPE)), Q_RANK ** -0.5),
        'c_kv_norm_g': 1.0 + nrm(ks[16], (DEPTH, KV_RANK), 0.01),
        'c_w_uk': nrm(ks[17], (DEPTH, KV_RANK, H_C, DH_NOPE), KV_RANK ** -0.5),
        'c_w_uv': nrm(ks[18], (DEPTH, KV_RANK, H_C, DH_V), KV_RANK ** -0.5),
        'w_out': nrm(ks[19], (DEPTH, D_MIX, D_MODEL), D_MIX ** -0.5),
        'final_norm_g': 1.0 + nrm(ks[20], (D_MODEL,), 0.01),
    }


def reference(x_prompt, x_sample, cache_a_conv, cache_b_k, cache_b_v, cache_c_latent, cache_c_krope,
              meta_tokens, norm_g, w_in, a_dw_w, a_dw_b, a_ln_g, a_ln_b, c_q_norm_g, c_w_uq,
              c_kv_norm_g, c_w_uk, c_w_uv, w_out, final_norm_g):
    Bp = x_prompt.shape[0]
    dt = x_prompt.dtype
    meta = jnp.broadcast_to(meta_tokens[None].astype(dt), (Bp, N_META, D_MODEL))
    h = jnp.concatenate([meta, x_prompt], axis=1)
    L = h.shape[1]
    p_pos = jnp.arange(L)
    p_chunk = jnp.where(p_pos < N_META, 0, 1 + (p_pos - N_META) // CHUNK)
    p_conv, p_bk, p_bv, p_lat, p_kr = [], [], [], [], []
    for l in range(DEPTH):
        conv0 = jnp.zeros((Bp, CONV_K - 1, D_A), dt)
        kv0 = jnp.zeros((Bp, 0, H_B, DH_B), dt)
        lat0 = jnp.zeros((Bp, 0, KV_RANK), dt)
        kr0 = jnp.zeros((Bp, 0, DH_ROPE), dt)
        h, cs, kn, vn, latn, krn = hybrid_layer(
            h, conv0, kv0, kv0, lat0, kr0, p_pos, p_pos, p_chunk, p_chunk,
            norm_g[l], w_in[l], a_dw_w[l], a_dw_b[l], a_ln_g[l], a_ln_b[l],
            c_q_norm_g[l], c_w_uq[l], c_kv_norm_g[l], c_w_uk[l], c_w_uv[l], w_out[l])
        p_conv.append(cs); p_bk.append(kn); p_bv.append(vn); p_lat.append(latn); p_kr.append(krn)
    y_prompt = rmsnorm(h, final_norm_g)[:, N_META:]

    Tn = x_sample.shape[1]
    past = cache_b_k.shape[2]
    s_qpos = past + jnp.arange(Tn)
    s_kpos = jnp.arange(past + Tn)
    s_qchunk = s_qpos // CHUNK
    s_kchunk = s_kpos // CHUNK
    g = x_sample
    s_conv, s_bk, s_bv, s_lat, s_kr = [], [], [], [], []
    for l in range(DEPTH):
        g, cs, kn, vn, latn, krn = hybrid_layer(
            g, cache_a_conv[l], cache_b_k[l], cache_b_v[l], cache_c_latent[l], cache_c_krope[l],
            s_qpos, s_kpos, s_qchunk, s_kchunk,
            norm_g[l], w_in[l], a_dw_w[l], a_dw_b[l], a_ln_g[l], a_ln_b[l],
            c_q_norm_g[l], c_w_uq[l], c_kv_norm_g[l], c_w_uk[l], c_w_uv[l], w_out[l])
        s_conv.append(cs); s_bk.append(kn); s_bv.append(vn); s_lat.append(latn); s_kr.append(krn)
    y_sample = rmsnorm(g, final_norm_g)

    return (y_prompt, y_sample,
            jnp.stack(p_conv), jnp.stack(p_bk), jnp.stack(p_bv), jnp.stack(p_lat), jnp.stack(p_kr),
            jnp.stack(s_conv), jnp.stack(s_bk), jnp.stack(s_bv), jnp.stack(s_lat), jnp.stack(s_kr))
```

```python
import functools

import jax
import jax.numpy as jnp
from jax import lax
from jax.experimental import pallas as pl
from jax.experimental.pallas import tpu as pltpu

CHUNK = 64
EPS = 1e-6
ROPE_BASE = 10000.0
NEG_INF = -1e30
TQ = 256
HALO = 32
LANE = 128
VMEM_LIMIT_BYTES = 56 * 1024 * 1024

F32 = jnp.float32
BF16 = jnp.bfloat16


def _params(n_axes):
    return pltpu.CompilerParams(dimension_semantics=("arbitrary",) * n_axes,
                                vmem_limit_bytes=VMEM_LIMIT_BYTES)


def _pick_tile(n, target, mult):
    best = None
    for t in range(mult, min(n, target) + 1, mult):
        if n % t == 0:
            best = t
    assert best is not None, (n, target, mult)
    return best


def _sigmoid(x):
    return 1.0 / (1.0 + jnp.exp(-x))


def _silu(x):
    return x * _sigmoid(x)


def _dot(a, b):
    return jnp.dot(a, b, preferred_element_type=F32)


def _dot_nt(a, b):
    return lax.dot_general(a, b, (((1,), (1,)), ((), ())), preferred_element_type=F32)


def _rmsnorm_kernel(x_ref, g_ref, o_ref):
    x = x_ref[...].astype(F32)
    ms = jnp.mean(x * x, axis=-1, keepdims=True)
    o_ref[...] = (x * lax.rsqrt(ms + EPS) * g_ref[...]).astype(o_ref.dtype)


def _rmsnorm(x, g, out_dtype, *, rows, row_off=0, col_block=0, tr=TQ):
    width = g.shape[-1]
    assert rows % tr == 0 and row_off % tr == 0
    off = row_off // tr
    return pl.pallas_call(
        _rmsnorm_kernel,
        grid=(rows // tr,),
        in_specs=[pl.BlockSpec((tr, width), lambda i: (i + off, col_block)),
                  pl.BlockSpec((1, width), lambda i: (0, 0))],
        out_specs=pl.BlockSpec((tr, width), lambda i: (i, 0)),
        out_shape=jax.ShapeDtypeStruct((rows, width), out_dtype),
        compiler_params=_params(1),
        name="rmsnorm",
    )(x, g.reshape(1, width).astype(F32))


def _matmul_kernel(x_ref, w_ref, o_ref, *, rc):
    def body(r, carry):
        rs = pl.multiple_of(r * rc, rc)
        o_ref[pl.ds(rs, rc), :] = _dot(x_ref[pl.ds(rs, rc), :], w_ref[...]).astype(o_ref.dtype)
        return carry
    lax.fori_loop(0, x_ref.shape[0] // rc, body, 0)


def _matmul(x, w, out_dtype):
    m, k = x.shape
    n = w.shape[1]
    tm = _pick_tile(m, 1280, 128)
    tn = _pick_tile(n, 1024, 128)
    rc = _pick_tile(tm, 640, 128)
    return pl.pallas_call(
        functools.partial(_matmul_kernel, rc=rc),
        grid=(m // tm, n // tn),
        in_specs=[pl.BlockSpec((tm, k), lambda i, j: (i, 0)),
                  pl.BlockSpec((k, tn), lambda i, j: (0, j))],
        out_specs=pl.BlockSpec((tm, tn), lambda i, j: (i, j)),
        out_shape=jax.ShapeDtypeStruct((m, n), out_dtype),
        compiler_params=_params(2),
        name="in_proj",
    )(x, w)


def _kr_kernel(x_ref, w_ref, cos_ref, sin_ref, o_ref):
    r = _dot(x_ref[...], w_ref[...])
    half = r.shape[1] // 2
    o_ref[...] = r[:, :half] * cos_ref[...] + r[:, half:] * sin_ref[...]


def _kr_proj(xn, w_kr2, cos, sin):
    m, k = xn.shape
    r2 = w_kr2.shape[1]
    tm = _pick_tile(m, 640, 128)
    return pl.pallas_call(
        _kr_kernel,
        grid=(m // tm,),
        in_specs=[pl.BlockSpec((tm, k), lambda i: (i, 0)),
                  pl.BlockSpec((k, r2), lambda i: (0, 0)),
                  pl.BlockSpec((tm, r2 // 2), lambda i: (i, 0)),
                  pl.BlockSpec((tm, r2 // 2), lambda i: (i, 0))],
        out_specs=pl.BlockSpec((tm, r2 // 2), lambda i: (i, 0)),
        out_shape=jax.ShapeDtypeStruct((m, r2 // 2), F32),
        compiler_params=_params(1),
        name="kr_proj",
    )(xn, w_kr2, cos, sin)


def _conv_kernel(val_ref, glu_ref, gate_ref, hval_ref, hglu_ref, w_ref, b_ref, lng_ref, lnb_ref,
                 ya_ref, u_ref, ubuf, cbuf, *, tile, taps, valid_from, halo_from_u):
    t = pl.program_id(0)
    ch = val_ref.shape[1]
    u = val_ref[...] * _sigmoid(glu_ref[...])
    if halo_from_u:
        row = t * tile + lax.broadcasted_iota(jnp.int32, (tile, 1), 0)
        u = jnp.where(row >= valid_from, u, 0.0)
        hu = hval_ref[...] * _sigmoid(hglu_ref[...])
        hrow = t * tile - HALO + lax.broadcasted_iota(jnp.int32, (HALO, 1), 0)
        hu = jnp.where(hrow >= valid_from, hu, 0.0)
    else:
        hu = hval_ref[0]
    u_ref[...] = u
    ubuf[0:HALO, :] = hu
    ubuf[HALO:HALO + tile, :] = u
    first = HALO - (taps - 1)
    for c in range(0, ch, LANE):
        acc = jnp.zeros((tile, LANE), F32) + b_ref[:, c:c + LANE]
        for k in range(taps):
            acc = acc + ubuf[first + k:first + k + tile, c:c + LANE] * w_ref[k:k + 1, c:c + LANE]
        cbuf[:, c:c + LANE] = acc
    y = cbuf[...]
    mu = jnp.mean(y, axis=-1, keepdims=True)
    yc = y - mu
    var = jnp.mean(yc * yc, axis=-1, keepdims=True)
    yn = yc * lax.rsqrt(var + EPS) * lng_ref[...] + lnb_ref[...]
    ya_ref[...] = (_silu(yn) * _silu(gate_ref[...])).astype(ya_ref.dtype)


def _conv_mixer(z, cols, halo, w, b, lng, lnb, *, tile, row_off, rows, valid_from, prev=None):
    mp = z.shape[0]
    ch = w.shape[1]
    taps = w.shape[0]
    assert taps - 1 <= HALO and tile % HALO == 0 and row_off % tile == 0 and rows % tile == 0
    c_val, c_glu, c_gate = (c // ch for c in cols)
    assert all(c % ch == 0 for c in cols)
    off = row_off // tile
    hb = tile // HALO
    halo_from_u = halo is None
    if halo_from_u:
        hspec_v = pl.BlockSpec((HALO, ch), lambda t: (jnp.maximum((t + off) * hb - 1, 0), c_val))
        hspec_g = pl.BlockSpec((HALO, ch), lambda t: (jnp.maximum((t + off) * hb - 1, 0), c_glu))
        hv, hg = z, z
    else:
        hspec_v = pl.BlockSpec((1, HALO, ch), lambda t: (t, 0, 0))
        hspec_g = pl.BlockSpec((1, HALO, ch), lambda t: (t, 0, 0))
        hv, hg = halo, halo
    row_spec = lambda cb: pl.BlockSpec((tile, ch), lambda t: (t + off, cb))
    vec_spec = pl.BlockSpec((1, ch), lambda t: (0, 0))
    out_specs = [pl.BlockSpec((tile, ch), lambda t: (t + off, 0)),
                 pl.BlockSpec((tile, ch), lambda t: (t + off, 0))]
    out_shape = [jax.ShapeDtypeStruct((mp, ch), BF16), jax.ShapeDtypeStruct((mp, ch), F32)]
    in_specs = [row_spec(c_val), row_spec(c_glu), row_spec(c_gate), hspec_v, hspec_g,
                pl.BlockSpec((taps, ch), lambda t: (0, 0)), vec_spec, vec_spec, vec_spec]
    args = [z, z, z, hv, hg, w, b.reshape(1, ch), lng.reshape(1, ch), lnb.reshape(1, ch)]
    aliases = {}
    kernel = functools.partial(_conv_kernel, tile=tile, taps=taps, valid_from=valid_from,
                               halo_from_u=halo_from_u)
    if prev is not None:
        in_specs += [pl.BlockSpec(memory_space=pl.ANY), pl.BlockSpec(memory_space=pl.ANY)]
        args += list(prev)
        aliases = {len(args) - 2: 0, len(args) - 1: 1}
        inner = kernel
        kernel = lambda *refs: inner(*refs[:9], *refs[11:])
    return pl.pallas_call(
        kernel,
        grid=(rows // tile,),
        in_specs=in_specs,
        out_specs=out_specs,
        out_shape=out_shape,
        scratch_shapes=[pltpu.VMEM((HALO + tile, ch), F32), pltpu.VMEM((tile, ch), F32)],
        input_output_aliases=aliases,
        compiler_params=_params(1),
        name="conv_mixer",
    )(*args)


def _strict_upper_sum_matrix(n):
    r = lax.broadcasted_iota(jnp.int32, (n, n), 0)
    c = lax.broadcasted_iota(jnp.int32, (n, n), 1)
    return jnp.where(r > c, 1.0, 0.0).astype(BF16)


def _sb_tile(zn, v, u_mat, carry, mask):
    lse = jnp.log(1.0 + jnp.exp(-jnp.abs(zn)))
    log_rest = jnp.minimum(zn, 0.0) - lse
    if mask is not None:
        log_rest = jnp.where(mask, log_rest, 0.0)
    hi = log_rest.astype(BF16)
    lo = (log_rest - hi.astype(F32)).astype(BF16)
    excl = _dot(hi, u_mat) + _dot(lo, u_mat)
    logit = (log_rest - zn) + excl + carry
    w = jnp.exp(logit)
    if mask is not None:
        w = jnp.where(mask, w, 0.0)
    out = _dot(w.astype(BF16), v)
    return out, carry + jnp.sum(log_rest, axis=1, keepdims=True)


def _sb_prompt_kernel(q_ref, k_ref, v_ref, g_ref, o_ref, acc_ref, carry_ref, *, scale, valid_from):
    qi = pl.program_id(1)
    tq = q_ref.shape[0]
    qn = (q_ref[...] * (-scale)).astype(BF16)
    u_mat = _strict_upper_sum_matrix(tq)
    row = lax.broadcasted_iota(jnp.int32, (tq, tq), 0)
    col = lax.broadcasted_iota(jnp.int32, (tq, tq), 1)

    def tile(j, mask):
        ks = pl.multiple_of(j * tq, tq)
        k = k_ref[pl.ds(ks, tq), :].astype(BF16)
        v = v_ref[pl.ds(ks, tq), :].astype(BF16)
        out, carry = _sb_tile(_dot_nt(qn, k), v, u_mat, carry_ref[...], mask)
        acc_ref[...] += out
        carry_ref[...] = carry

    acc_ref[...] = jnp.zeros_like(acc_ref)
    carry_ref[...] = jnp.zeros_like(carry_ref)
    tile(qi, (col < row) & (col + qi * tq >= valid_from))

    def body(i, c):
        tile(qi - 1 - i, None)
        return c
    lax.fori_loop(0, jnp.maximum(qi - 1, 0), body, 0)

    @pl.when(qi >= 1)
    def _():
        tile(0, col >= valid_from)

    o_ref[...] = (acc_ref[...] * _silu(g_ref[...])).astype(o_ref.dtype)


def _sb_prompt(z, c_q, c_k, c_v, c_gate, n_heads, dh, *, rows, valid_from):
    mp = z.shape[0]
    assert rows % TQ == 0 and all(c % dh == 0 for c in (c_q, c_k, c_v, c_gate)) and dh == LANE
    bq, bk, bv, bg = (c // dh for c in (c_q, c_k, c_v, c_gate))
    return pl.pallas_call(
        functools.partial(_sb_prompt_kernel, scale=dh ** -0.5, valid_from=valid_from),
        grid=(n_heads, rows // TQ),
        in_specs=[pl.BlockSpec((TQ, dh), lambda h, i: (i, bq + h)),
                  pl.BlockSpec((rows, dh), lambda h, i: (0, bk + h)),
                  pl.BlockSpec((rows, dh), lambda h, i: (0, bv + h)),
                  pl.BlockSpec((TQ, dh), lambda h, i: (i, bg + h))],
        out_specs=pl.BlockSpec((TQ, dh), lambda h, i: (i, h)),
        out_shape=jax.ShapeDtypeStruct((mp, n_heads * dh), BF16),
        scratch_shapes=[pltpu.VMEM((TQ, dh), F32), pltpu.VMEM((TQ, 1), F32)],
        compiler_params=_params(2),
        name="sb_prompt",
    )(z, z, z, z)


def _sb_sample_kernel(q_ref, kn_ref, vn_ref, kc_ref, vc_ref, g_ref, yb_hbm, o_ref, *, scale, tk):
    del yb_hbm
    tq = q_ref.shape[0]
    past = kc_ref.shape[1]
    qn = (q_ref[...] * (-scale)).astype(BF16)
    tn = max(tq, LANE)
    zpad = jnp.zeros((tn - tq, kn_ref.shape[1]), F32)
    kn = jnp.concatenate([kn_ref[...], zpad], axis=0).astype(BF16)
    vn = jnp.concatenate([vn_ref[...], zpad], axis=0).astype(BF16)
    row = lax.broadcasted_iota(jnp.int32, (tq, tn), 0)
    col = lax.broadcasted_iota(jnp.int32, (tq, tn), 1)
    acc, carry = _sb_tile(_dot_nt(qn, kn), vn, _strict_upper_sum_matrix(tn),
                          jnp.zeros((tq, 1), F32), col < row)
    u_mat = _strict_upper_sum_matrix(tk)

    def body(i, c):
        acc, carry = c
        ks = pl.multiple_of(past - (i + 1) * tk, tk)
        k = kc_ref[0, pl.ds(ks, tk), :].astype(BF16)
        v = vc_ref[0, pl.ds(ks, tk), :].astype(BF16)
        out, carry = _sb_tile(_dot_nt(qn, k), v, u_mat, carry, None)
        return acc + out, carry
    acc, carry = lax.fori_loop(0, past // tk, body, (acc, carry))
    o_ref[...] = (acc * _silu(g_ref[...])).astype(o_ref.dtype)


def _sb_sample(z, cache_k, cache_v, yb, c_q, c_k, c_v, c_gate, n_heads, dh, *, row_off, t_new):
    nb, past = cache_k.shape[0], cache_k.shape[1]
    assert row_off % t_new == 0 and dh == LANE
    tk = _pick_tile(past, TQ, LANE)
    rb = row_off // t_new
    bq, bk, bv, bg = (c // dh for c in (c_q, c_k, c_v, c_gate))
    row_spec = lambda cb: pl.BlockSpec((t_new, dh), lambda b, h: (rb + b, cb + h))
    cache_spec = pl.BlockSpec((1, past, dh), lambda b, h: (b, 0, h))
    return pl.pallas_call(
        functools.partial(_sb_sample_kernel, scale=dh ** -0.5, tk=tk),
        grid=(nb, n_heads),
        in_specs=[row_spec(bq), row_spec(bk), row_spec(bv), cache_spec, cache_spec, row_spec(bg),
                  pl.BlockSpec(memory_space=pl.ANY)],
        out_specs=pl.BlockSpec((t_new, dh), lambda b, h: (rb + b, h)),
        out_shape=jax.ShapeDtypeStruct(yb.shape, yb.dtype),
        input_output_aliases={6: 0},
        compiler_params=_params(2),
        name="sb_sample",
    )(z, z, z, cache_k.reshape(nb, past, n_heads * dh), cache_v.reshape(nb, past, n_heads * dh), z, yb)


def _q_up_kernel(cq_ref, w_ref, cos_ref, sin_ref, o_ref, *, scale, d_nope):
    cq = cq_ref[...]
    cos = cos_ref[...]
    sin = sin_ref[...]
    d_rope = cos.shape[1]
    for h in range(w_ref.shape[0]):
        r = _dot(cq, w_ref[h])
        rope = r[:, d_nope:d_nope + d_rope] * cos + r[:, d_nope + d_rope:] * sin
        o_ref[h, :, 0:d_nope] = (r[:, :d_nope] * scale).astype(o_ref.dtype)
        o_ref[h, :, d_nope:] = (rope * scale).astype(o_ref.dtype)


def _q_up(cq, wq, cos, sin, *, d_nope, scale):
    m, qr = cq.shape
    nh, _, wcols = wq.shape
    d_rope = cos.shape[1]
    assert wcols == d_nope + 2 * d_rope
    tm = _pick_tile(m, 640, 128)
    return pl.pallas_call(
        functools.partial(_q_up_kernel, scale=scale, d_nope=d_nope),
        grid=(m // tm,),
        in_specs=[pl.BlockSpec((tm, qr), lambda i: (i, 0)),
                  pl.BlockSpec((nh, qr, wcols), lambda i: (0, 0, 0)),
                  pl.BlockSpec((tm, d_rope), lambda i: (i, 0)),
                  pl.BlockSpec((tm, d_rope), lambda i: (i, 0))],
        out_specs=pl.BlockSpec((nh, tm, d_nope + d_rope), lambda i: (0, i, 0)),
        out_shape=jax.ShapeDtypeStruct((nh, m, d_nope + d_rope), BF16),
        compiler_params=_params(1),
        name="q_up",
    )(cq, wq, cos, sin)


def _kv_up_kernel(lat_ref, kr_ref, w_ref, k_ref, v_ref, *, d_nope):
    lat = lat_ref[...].astype(BF16)
    kr = kr_ref[...].astype(BF16)
    for h in range(w_ref.shape[0]):
        r = _dot(lat, w_ref[h])
        k_ref[h, :, 0:d_nope] = r[:, :d_nope].astype(BF16)
        k_ref[h, :, d_nope:] = kr
        v_ref[h] = r[:, d_nope:].astype(BF16)


def _kv_up(lat, kr, wkv, *, rows, d_nope):
    kvr = lat.shape[1]
    d_rope = kr.shape[1]
    nh, _, wcols = wkv.shape
    d_v = wcols - d_nope
    tr = _pick_tile(rows, 768, TQ)
    return pl.pallas_call(
        functools.partial(_kv_up_kernel, d_nope=d_nope),
        grid=(rows // tr,),
        in_specs=[pl.BlockSpec((tr, kvr), lambda i: (i, 0)),
                  pl.BlockSpec((tr, d_rope), lambda i: (i, 0)),
                  pl.BlockSpec((nh, kvr, wcols), lambda i: (0, 0, 0))],
        out_specs=[pl.BlockSpec((nh, tr, d_nope + d_rope), lambda i: (0, i, 0)),
                   pl.BlockSpec((nh, tr, d_v), lambda i: (0, i, 0))],
        out_shape=[jax.ShapeDtypeStruct((nh, rows, d_nope + d_rope), BF16),
                   jax.ShapeDtypeStruct((nh, rows, d_v), BF16)],
        compiler_params=_params(1),
        name="kv_up",
    )(lat, kr, wkv)


def _mla_prompt_kernel(q_ref, k_ref, v_ref, g_ref, o_ref, m_ref, l_ref, acc_ref, *, valid_from):
    qi = pl.program_id(1)
    tq = q_ref.shape[1]
    q = q_ref[0]
    row = lax.broadcasted_iota(jnp.int32, (tq, tq), 0)
    col = lax.broadcasted_iota(jnp.int32, (tq, tq), 1)

    def tile(j, mask):
        ks = pl.multiple_of(j * tq, tq)
        s = _dot_nt(q, k_ref[0, pl.ds(ks, tq), :])
        if mask is not None:
            s = jnp.where(mask, s, NEG_INF)
        m_old = m_ref[...]
        m_new = jnp.maximum(m_old, jnp.max(s, axis=1, keepdims=True))
        alpha = jnp.exp(m_old - m_new)
        p = jnp.exp(s - m_new)
        l_ref[...] = alpha * l_ref[...] + jnp.sum(p, axis=1, keepdims=True)
        acc_ref[...] = alpha * acc_ref[...] + _dot(p.astype(BF16), v_ref[0, pl.ds(ks, tq), :])
        m_ref[...] = m_new

    m_ref[...] = jnp.full_like(m_ref, NEG_INF)
    l_ref[...] = jnp.zeros_like(l_ref)
    acc_ref[...] = jnp.zeros_like(acc_ref)
    tile(0, col >= valid_from)

    def body(j, c):
        tile(j, None)
        return c
    lax.fori_loop(1, qi, body, 0)

    @pl.when(qi >= 1)
    def _():
        tile(qi, (col // CHUNK) <= (row // CHUNK))

    o_ref[...] = (acc_ref[...] / l_ref[...] * _silu(g_ref[...])).astype(o_ref.dtype)


def _mla_prompt(q_cat, k_cat, v, z, c_gate, *, rows, valid_from):
    nh, mp, dk = q_cat.shape
    dv = v.shape[2]
    assert rows % TQ == 0 and c_gate % dv == 0 and dv == LANE and TQ % CHUNK == 0
    bg = c_gate // dv
    return pl.pallas_call(
        functools.partial(_mla_prompt_kernel, valid_from=valid_from),
        grid=(nh, rows // TQ),
        in_specs=[pl.BlockSpec((1, TQ, dk), lambda h, i: (h, i, 0)),
                  pl.BlockSpec((1, rows, dk), lambda h, i: (h, 0, 0)),
                  pl.BlockSpec((1, rows, dv), lambda h, i: (h, 0, 0)),
                  pl.BlockSpec((TQ, dv), lambda h, i: (i, bg + h))],
        out_specs=pl.BlockSpec((TQ, dv), lambda h, i: (i, h)),
        out_shape=jax.ShapeDtypeStruct((mp, nh * dv), BF16),
        scratch_shapes=[pltpu.VMEM((TQ, 1), F32), pltpu.VMEM((TQ, 1), F32), pltpu.VMEM((TQ, dv), F32)],
        compiler_params=_params(2),
        name="mla_prompt",
    )(q_cat, k_cat, v, z)


def _mla_sample_kernel(q_ref, latc_ref, krc_ref, latn_ref, krn_ref, w_ref, g_ref, yc_hbm, o_ref,
                       ql_ref, qr_ref, *, d_nope):
    del yc_hbm
    nh, t_new, _ = q_ref.shape
    latc = latc_ref[0].astype(BF16)
    krc = krc_ref[0].astype(BF16)
    latn = latn_ref[...].astype(BF16)
    krn = krn_ref[...].astype(BF16)
    for h in range(nh):
        qh = q_ref[h]
        ql_ref[h * t_new:(h + 1) * t_new, :] = _dot_nt(qh[:, :d_nope], w_ref[h, :, :d_nope]).astype(BF16)
        qr_ref[h * t_new:(h + 1) * t_new, :] = qh[:, d_nope:]
    ql = ql_ref[...]
    qr = qr_ref[...]
    s_c = _dot_nt(ql, latc) + _dot_nt(qr, krc)
    s_n = _dot_nt(ql, latn) + _dot_nt(qr, krn)
    m = jnp.maximum(jnp.max(s_c, axis=1, keepdims=True), jnp.max(s_n, axis=1, keepdims=True))
    p_c = jnp.exp(s_c - m)
    p_n = jnp.exp(s_n - m)
    denom = jnp.sum(p_c, axis=1, keepdims=True) + jnp.sum(p_n, axis=1, keepdims=True)
    o_lat = ((_dot(p_c.astype(BF16), latc) + _dot(p_n.astype(BF16), latn)) / denom).astype(BF16)
    d_v = w_ref.shape[2] - d_nope
    for h in range(nh):
        yh = _dot(o_lat[h * t_new:(h + 1) * t_new, :], w_ref[h, :, d_nope:])
        o_ref[:, h * d_v:(h + 1) * d_v] = (yh * _silu(g_ref[:, h * d_v:(h + 1) * d_v])).astype(o_ref.dtype)


def _mla_sample(q_cat, lat_cache, kr_cache, lat_new, kr_new, wkv, z, yc, c_gate, *, row_off, t_new, d_nope):
    nh, mp, dk = q_cat.shape
    nb, past, kvr = lat_cache.shape
    d_rope = kr_cache.shape[2]
    d_v = wkv.shape[2] - d_nope
    width = nh * d_v
    assert row_off % t_new == 0 and c_gate % width == 0
    rb = row_off // t_new
    return pl.pallas_call(
        functools.partial(_mla_sample_kernel, d_nope=d_nope),
        grid=(nb,),
        in_specs=[pl.BlockSpec((nh, t_new, dk), lambda b: (0, rb + b, 0)),
                  pl.BlockSpec((1, past, kvr), lambda b: (b, 0, 0)),
                  pl.BlockSpec((1, past, d_rope), lambda b: (b, 0, 0)),
                  pl.BlockSpec((t_new, kvr), lambda b: (rb + b, 0)),
                  pl.BlockSpec((t_new, d_rope), lambda b: (rb + b, 0)),
                  pl.BlockSpec(wkv.shape, lambda b: (0, 0, 0)),
                  pl.BlockSpec((t_new, width), lambda b: (rb + b, c_gate // width)),
                  pl.BlockSpec(memory_space=pl.ANY)],
        out_specs=pl.BlockSpec((t_new, width), lambda b: (rb + b, 0)),
        out_shape=jax.ShapeDtypeStruct(yc.shape, yc.dtype),
        scratch_shapes=[pltpu.VMEM((nh * t_new, kvr), BF16), pltpu.VMEM((nh * t_new, d_rope), BF16)],
        input_output_aliases={7: 0},
        compiler_params=_params(1),
        name="mla_sample",
    )(q_cat, lat_cache, kr_cache, lat_new, kr_new, wkv, z, yc)


def _out_proj_kernel(ya_ref, yb_ref, yc_ref, wa_ref, wb_ref, wc_ref, h_ref, o_ref, *, rc):
    def body(r, carry):
        rs = pl.multiple_of(r * rc, rc)
        sl = pl.ds(rs, rc)
        y = _dot(ya_ref[sl, :], wa_ref[...]) + _dot(yb_ref[sl, :], wb_ref[...]) + _dot(yc_ref[sl, :], wc_ref[...])
        o_ref[sl, :] = h_ref[sl, :] + y
        return carry
    lax.fori_loop(0, ya_ref.shape[0] // rc, body, 0)


def _out_proj(ya, yb, yc, w_out, h):
    m, d = h.shape
    da, db, dc = ya.shape[1], yb.shape[1], yc.shape[1]
    assert da == db and dc % da == 0 and w_out.shape == (da + db + dc, d)
    tm = _pick_tile(m, 1280, 128)
    tn = _pick_tile(d, 512, 128)
    rc = _pick_tile(tm, 640, 128)
    return pl.pallas_call(
        functools.partial(_out_proj_kernel, rc=rc),
        grid=(m // tm, d // tn),
        in_specs=[pl.BlockSpec((tm, da), lambda i, j: (i, 0)),
                  pl.BlockSpec((tm, db), lambda i, j: (i, 0)),
                  pl.BlockSpec((tm, dc), lambda i, j: (i, 0)),
                  pl.BlockSpec((da, tn), lambda i, j: (0, j)),
                  pl.BlockSpec((db, tn), lambda i, j: (1, j)),
                  pl.BlockSpec((dc, tn), lambda i, j: ((da + db) // dc, j)),
                  pl.BlockSpec((tm, tn), lambda i, j: (i, j))],
        out_specs=pl.BlockSpec((tm, tn), lambda i, j: (i, j)),
        out_shape=jax.ShapeDtypeStruct((m, d), F32),
        compiler_params=_params(2),
        name="out_proj",
    )(ya, yb, yc, w_out, w_out, w_out, h)


def _swap_neg_halves(w):
    half = w.shape[-1] // 2
    return jnp.concatenate([-w[..., half:], w[..., :half]], axis=-1)


def kernel(x_prompt, x_sample, cache_a_conv, cache_b_k, cache_b_v, cache_c_latent, cache_c_krope, meta_tokens, norm_g, w_in, a_dw_w, a_dw_b, a_ln_g, a_ln_b, c_q_norm_g, c_w_uq, c_kv_norm_g, c_w_uk, c_w_uv, w_out, final_norm_g):
    bp, seq, d = x_prompt.shape
    nb, t_new, _ = x_sample.shape
    depth = w_in.shape[0]
    n_meta = meta_tokens.shape[0]
    hist, d_a = cache_a_conv.shape[2], cache_a_conv.shape[3]
    past, h_b, dh_b = cache_b_k.shape[2], cache_b_k.shape[3], cache_b_k.shape[4]
    d_b = h_b * dh_b
    kvr = cache_c_latent.shape[3]
    d_rope = cache_c_krope.shape[3]
    qr = c_q_norm_g.shape[1]
    h_c, d_nope = c_w_uk.shape[2], c_w_uk.shape[3]
    d_v = c_w_uv.shape[3]
    d_c = h_c * d_v
    assert bp == 1 and n_meta <= TQ and n_meta % 8 == 0 and seq % TQ == 0
    assert t_new == CHUNK and past % CHUNK == 0 and hist == a_dw_w.shape[1] - 1 and hist <= t_new
    assert d_nope == LANE and d_v == LANE and dh_b == LANE

    pad = TQ - n_meta
    p_end = TQ + seq
    s0 = p_end
    mp = s0 + nb * t_new
    assert (nb * t_new) % LANE == 0

    o_in = {}
    acc = 0
    for name, width in (("a_val", d_a), ("a_glu", d_a), ("a_gate", d_a), ("b_q", d_b), ("b_k", d_b),
                        ("b_v", d_b), ("b_gate", d_b), ("c_q", qr), ("c_kv", kvr), ("c_kr", d_rope),
                        ("c_gate", d_c)):
        o_in[name] = (acc, width)
        acc += width
    assert acc == w_in.shape[2]
    order = ("c_q", "c_kv", "c_gate", "a_val", "a_glu", "a_gate", "b_q", "b_k", "b_v", "b_gate")
    col = {}
    acc = 0
    for name in order:
        col[name] = acc
        acc += o_in[name][1]
    n_main = acc
    assert col["c_q"] % qr == 0 and col["c_kv"] % kvr == 0

    def w_cols(name):
        a, wd = o_in[name]
        return w_in[:, :, a:a + wd]

    w_main = jnp.concatenate([w_cols(n) for n in order], axis=2).astype(BF16)
    w_kr = w_cols("c_kr")
    w_kr2 = jnp.concatenate([w_kr, _swap_neg_halves(w_kr)], axis=2).astype(BF16)
    wq4 = c_w_uq.reshape(depth, qr, h_c, d_nope + d_rope)
    wq_rope = wq4[..., d_nope:]
    wq = jnp.concatenate([wq4[..., :d_nope], wq_rope, _swap_neg_halves(wq_rope)], axis=-1)
    wq = jnp.transpose(wq, (0, 2, 1, 3)).astype(BF16)
    wkv = jnp.concatenate([jnp.transpose(c_w_uk, (0, 2, 1, 3)), jnp.transpose(c_w_uv, (0, 2, 1, 3))],
                          axis=-1).astype(BF16)
    w_out_b = w_out.astype(BF16)

    pos_prompt = jnp.maximum(jnp.arange(p_end) - pad, 0)
    pos_sample = past + (jnp.arange(nb * t_new) % t_new)
    pos = jnp.concatenate([pos_prompt, pos_sample]).astype(F32)
    half = d_rope // 2
    inv_freq = ROPE_BASE ** (-jnp.arange(half, dtype=F32) / half)
    ang = pos[:, None] * inv_freq[None, :]
    cos = jnp.tile(jnp.cos(ang), (1, 2))
    sin = jnp.tile(jnp.sin(ang), (1, 2))

    h = jnp.concatenate([jnp.zeros((pad, d), F32), meta_tokens.astype(F32), x_prompt[0],
                         x_sample.reshape(nb * t_new, d)], axis=0)
    conv_halo = jnp.pad(cache_a_conv, ((0, 0), (0, 0), (HALO - hist, 0), (0, 0)))
    mla_scale = (d_nope + d_rope) ** -0.5

    outs = {k: [] for k in ("p_conv", "p_bk", "p_bv", "p_lat", "p_kr", "s_conv", "s_bk", "s_bv", "s_lat", "s_kr")}
    for l in range(depth):
        xn = _rmsnorm(h, norm_g[l], BF16, rows=mp)
        z = _matmul(xn, w_main[l], F32)
        kr_new = _kr_proj(xn, w_kr2[l], cos, sin)

        cols_a = (col["a_val"], col["a_glu"], col["a_gate"])
        ya, u = _conv_mixer(z, cols_a, None, a_dw_w[l], a_dw_b[l], a_ln_g[l], a_ln_b[l],
                            tile=TQ, row_off=0, rows=p_end, valid_from=pad)
        ya, u = _conv_mixer(z, cols_a, conv_halo[l], a_dw_w[l], a_dw_b[l], a_ln_g[l], a_ln_b[l],
                            tile=t_new, row_off=s0, rows=nb * t_new, valid_from=0, prev=(ya, u))

        yb = _sb_prompt(z, col["b_q"], col["b_k"], col["b_v"], col["b_gate"], h_b, dh_b,
                        rows=p_end, valid_from=pad)
        yb = _sb_sample(z, cache_b_k[l], cache_b_v[l], yb, col["b_q"], col["b_k"], col["b_v"],
                        col["b_gate"], h_b, dh_b, row_off=s0, t_new=t_new)

        cq = _rmsnorm(z, c_q_norm_g[l], BF16, rows=mp, col_block=col["c_q"] // qr)
        lat_new = _rmsnorm(z, c_kv_norm_g[l], F32, rows=mp, col_block=col["c_kv"] // kvr)
        q_cat = _q_up(cq, wq[l], cos, sin, d_nope=d_nope, scale=mla_scale)
        k_cat, v_up = _kv_up(lat_new, kr_new, wkv[l], rows=p_end, d_nope=d_nope)
        yc = _mla_prompt(q_cat, k_cat, v_up, z, col["c_gate"], rows=p_end, valid_from=pad)
        yc = _mla_sample(q_cat, cache_c_latent[l], cache_c_krope[l], lat_new, kr_new, wkv[l], z, yc,
                         col["c_gate"], row_off=s0, t_new=t_new, d_nope=d_nope)

        h = _out_proj(ya, yb, yc, w_out_b[l], h)

        k_new = z[:, col["b_k"]:col["b_k"] + d_b]
        v_new = z[:, col["b_v"]:col["b_v"] + d_b]
        outs["p_conv"].append(u[p_end - hist:p_end].reshape(1, hist, d_a))
        outs["p_bk"].append(k_new[pad:p_end].reshape(1, n_meta + seq, h_b, dh_b))
        outs["p_bv"].append(v_new[pad:p_end].reshape(1, n_meta + seq, h_b, dh_b))
        outs["p_lat"].append(lat_new[pad:p_end].reshape(1, n_meta + seq, kvr))
        outs["p_kr"].append(kr_new[pad:p_end].reshape(1, n_meta + seq, d_rope))
        outs["s_conv"].append(u[s0:].reshape(nb, t_new, d_a)[:, t_new - hist:])
        outs["s_bk"].append(k_new[s0:].reshape(nb, t_new, h_b, dh_b))
        outs["s_bv"].append(v_new[s0:].reshape(nb, t_new, h_b, dh_b))
        outs["s_lat"].append(lat_new[s0:].reshape(nb, t_new, kvr))
        outs["s_kr"].append(kr_new[s0:].reshape(nb, t_new, d_rope))

    y_prompt = _rmsnorm(h, final_norm_g, F32, rows=seq, row_off=TQ).reshape(1, seq, d)
    y_sample = _rmsnorm(h, final_norm_g, F32, rows=nb * t_new, row_off=s0, tr=t_new).reshape(nb, t_new, d)
    st = lambda k: jnp.stack(outs[k])
    return (y_prompt, y_sample, st("p_conv"), st("p_bk"), st("p_bv"), st("p_lat"), st("p_kr"),
            st("s_conv"), st("s_bk"), st("s_bv"), st("s_lat"), st("s_kr"))
```

```python
import functools

import jax
import jax.numpy as jnp
from jax import lax
from jax.experimental import pallas as pl
from jax.experimental.pallas import tpu as pltpu

CHUNK = 64
EPS = 1e-6
ROPE_BASE = 10000.0
NEG_INF = -1e30
LOG2E = 1.4426950408889634
TQ = 256
ATTN_CHUNK = 768
HALO = 32
LANE = 128
VMEM_LIMIT_BYTES = 56 * 1024 * 1024

F32 = jnp.float32
BF16 = jnp.bfloat16


def _params(n_axes):
    return pltpu.CompilerParams(dimension_semantics=("arbitrary",) * n_axes,
                                vmem_limit_bytes=VMEM_LIMIT_BYTES)


def _pick_tile(n, target, mult):
    best = None
    for t in range(mult, min(n, target) + 1, mult):
        if n % t == 0:
            best = t
    assert best is not None, (n, target, mult)
    return best


def _sigmoid(x):
    return 1.0 / (1.0 + jnp.exp(-x))


def _silu(x):
    return x * _sigmoid(x)


def _dot(a, b):
    return jnp.dot(a, b, preferred_element_type=F32)


def _dot_nt(a, b):
    return lax.dot_general(a, b, (((1,), (1,)), ((), ())), preferred_element_type=F32)


def _row_reduce(x, op, reduce):
    folded = x[:, :LANE]
    for g in range(1, x.shape[1] // LANE):
        folded = op(folded, x[:, g * LANE:(g + 1) * LANE])
    return jnp.broadcast_to(reduce(folded, axis=1, keepdims=True), folded.shape)


def _rep(x, width):
    return x if width == LANE else jnp.concatenate([x] * (width // LANE), axis=1)


def _rmsnorm_kernel(x_ref, g_ref, o_ref):
    x = x_ref[...].astype(F32)
    ms = jnp.mean(x * x, axis=-1, keepdims=True)
    o_ref[...] = (x * lax.rsqrt(ms + EPS) * g_ref[...]).astype(o_ref.dtype)


def _rmsnorm(x, g, out_dtype, *, rows, row_off=0, col_block=0, tr=TQ):
    width = g.shape[-1]
    assert rows % tr == 0 and row_off % tr == 0
    off = row_off // tr
    return pl.pallas_call(
        _rmsnorm_kernel,
        grid=(rows // tr,),
        in_specs=[pl.BlockSpec((tr, width), lambda i: (i + off, col_block)),
                  pl.BlockSpec((1, width), lambda i: (0, 0))],
        out_specs=pl.BlockSpec((tr, width), lambda i: (i, 0)),
        out_shape=jax.ShapeDtypeStruct((rows, width), out_dtype),
        compiler_params=_params(1),
        name="rmsnorm",
    )(x, g.reshape(1, width).astype(F32))


def _matmul_kernel(x_ref, w_ref, o_ref, *, rc):
    def body(r, carry):
        rs = pl.multiple_of(r * rc, rc)
        o_ref[pl.ds(rs, rc), :] = _dot(x_ref[pl.ds(rs, rc), :], w_ref[...]).astype(o_ref.dtype)
        return carry
    lax.fori_loop(0, x_ref.shape[0] // rc, body, 0)


def _matmul(x, w, out_dtype):
    m, k = x.shape
    n = w.shape[1]
    tm = _pick_tile(m, 1280, 128)
    tn = _pick_tile(n, 1024, 128)
    rc = _pick_tile(tm, 640, 128)
    return pl.pallas_call(
        functools.partial(_matmul_kernel, rc=rc),
        grid=(m // tm, n // tn),
        in_specs=[pl.BlockSpec((tm, k), lambda i, j: (i, 0)),
                  pl.BlockSpec((k, tn), lambda i, j: (0, j))],
        out_specs=pl.BlockSpec((tm, tn), lambda i, j: (i, j)),
        out_shape=jax.ShapeDtypeStruct((m, n), out_dtype),
        compiler_params=_params(2),
        name="in_proj",
    )(x, w)


def _kr_kernel(x_ref, w_ref, cos_ref, sin_ref, o_ref):
    r = _dot(x_ref[...], w_ref[...])
    half = r.shape[1] // 2
    o_ref[...] = r[:, :half] * cos_ref[...] + r[:, half:] * sin_ref[...]


def _kr_proj(xn, w_kr2, cos, sin):
    m, k = xn.shape
    r2 = w_kr2.shape[1]
    tm = _pick_tile(m, 640, 128)
    return pl.pallas_call(
        _kr_kernel,
        grid=(m // tm,),
        in_specs=[pl.BlockSpec((tm, k), lambda i: (i, 0)),
                  pl.BlockSpec((k, r2), lambda i: (0, 0)),
                  pl.BlockSpec((tm, r2 // 2), lambda i: (i, 0)),
                  pl.BlockSpec((tm, r2 // 2), lambda i: (i, 0))],
        out_specs=pl.BlockSpec((tm, r2 // 2), lambda i: (i, 0)),
        out_shape=jax.ShapeDtypeStruct((m, r2 // 2), F32),
        compiler_params=_params(1),
        name="kr_proj",
    )(xn, w_kr2, cos, sin)


def _conv_kernel(val_ref, glu_ref, gate_ref, hval_ref, hglu_ref, w_ref, b_ref, lng_ref, lnb_ref,
                 ya_ref, u_ref, ubuf, cbuf, *, tile, taps, valid_from, halo_from_u):
    t = pl.program_id(0)
    ch = val_ref.shape[1]
    u = val_ref[...] * _sigmoid(glu_ref[...])
    if halo_from_u:
        row = t * tile + lax.broadcasted_iota(jnp.int32, (tile, 1), 0)
        u = jnp.where(row >= valid_from, u, 0.0)
        hu = hval_ref[...] * _sigmoid(hglu_ref[...])
        hrow = t * tile - HALO + lax.broadcasted_iota(jnp.int32, (HALO, 1), 0)
        hu = jnp.where(hrow >= valid_from, hu, 0.0)
    else:
        hu = hval_ref[0]
    u_ref[...] = u
    ubuf[0:HALO, :] = hu
    ubuf[HALO:HALO + tile, :] = u
    first = HALO - (taps - 1)
    for c in range(0, ch, LANE):
        acc = jnp.zeros((tile, LANE), F32) + b_ref[:, c:c + LANE]
        for k in range(taps):
            acc = acc + ubuf[first + k:first + k + tile, c:c + LANE] * w_ref[k:k + 1, c:c + LANE]
        cbuf[:, c:c + LANE] = acc
    y = cbuf[...]
    mu = jnp.mean(y, axis=-1, keepdims=True)
    yc = y - mu
    var = jnp.mean(yc * yc, axis=-1, keepdims=True)
    yn = yc * lax.rsqrt(var + EPS) * lng_ref[...] + lnb_ref[...]
    ya_ref[...] = (_silu(yn) * _silu(gate_ref[...])).astype(ya_ref.dtype)


def _conv_mixer(z, cols, halo, w, b, lng, lnb, *, tile, row_off, rows, valid_from, prev=None):
    mp = z.shape[0]
    ch = w.shape[1]
    taps = w.shape[0]
    assert taps - 1 <= HALO and tile % HALO == 0 and row_off % tile == 0 and rows % tile == 0
    c_val, c_glu, c_gate = (c // ch for c in cols)
    assert all(c % ch == 0 for c in cols)
    off = row_off // tile
    hb = tile // HALO
    halo_from_u = halo is None
    if halo_from_u:
        hspec_v = pl.BlockSpec((HALO, ch), lambda t: (jnp.maximum((t + off) * hb - 1, 0), c_val))
        hspec_g = pl.BlockSpec((HALO, ch), lambda t: (jnp.maximum((t + off) * hb - 1, 0), c_glu))
        hv, hg = z, z
    else:
        hspec_v = pl.BlockSpec((1, HALO, ch), lambda t: (t, 0, 0))
        hspec_g = pl.BlockSpec((1, HALO, ch), lambda t: (t, 0, 0))
        hv, hg = halo, halo
    row_spec = lambda cb: pl.BlockSpec((tile, ch), lambda t: (t + off, cb))
    vec_spec = pl.BlockSpec((1, ch), lambda t: (0, 0))
    out_specs = [pl.BlockSpec((tile, ch), lambda t: (t + off, 0)),
                 pl.BlockSpec((tile, ch), lambda t: (t + off, 0))]
    out_shape = [jax.ShapeDtypeStruct((mp, ch), BF16), jax.ShapeDtypeStruct((mp, ch), F32)]
    in_specs = [row_spec(c_val), row_spec(c_glu), row_spec(c_gate), hspec_v, hspec_g,
                pl.BlockSpec((taps, ch), lambda t: (0, 0)), vec_spec, vec_spec, vec_spec]
    args = [z, z, z, hv, hg, w, b.reshape(1, ch), lng.reshape(1, ch), lnb.reshape(1, ch)]
    aliases = {}
    kernel = functools.partial(_conv_kernel, tile=tile, taps=taps, valid_from=valid_from,
                               halo_from_u=halo_from_u)
    if prev is not None:
        in_specs += [pl.BlockSpec(memory_space=pl.ANY), pl.BlockSpec(memory_space=pl.ANY)]
        args += list(prev)
        aliases = {len(args) - 2: 0, len(args) - 1: 1}
        inner = kernel
        kernel = lambda *refs: inner(*refs[:9], *refs[11:])
    return pl.pallas_call(
        kernel,
        grid=(rows // tile,),
        in_specs=in_specs,
        out_specs=out_specs,
        out_shape=out_shape,
        scratch_shapes=[pltpu.VMEM((HALO + tile, ch), F32), pltpu.VMEM((tile, ch), F32)],
        input_output_aliases=aliases,
        compiler_params=_params(1),
        name="conv_mixer",
    )(*args)


def _strict_upper_sum_matrix(n):
    r = lax.broadcasted_iota(jnp.int32, (n, n), 0)
    c = lax.broadcasted_iota(jnp.int32, (n, n), 1)
    return jnp.where(r > c, 1.0, 0.0).astype(BF16)


def _sb_tile(zn, v, u_mat, carry, mask):
    lse = jnp.log(1.0 + jnp.exp(-jnp.abs(zn)))
    log_rest = jnp.minimum(zn, 0.0) - lse
    if mask is not None:
        log_rest = jnp.where(mask, log_rest, 0.0)
    hi = log_rest.astype(BF16)
    lo = (log_rest - hi.astype(F32)).astype(BF16)
    excl = _dot(hi, u_mat) + _dot(lo, u_mat)
    logit = (log_rest - zn) + excl + _rep(carry, zn.shape[1])
    w = jnp.exp(logit)
    if mask is not None:
        w = jnp.where(mask, w, 0.0)
    out = _dot(w.astype(BF16), v)
    return out, carry + _row_reduce(log_rest, jnp.add, jnp.sum)


def _sb_chunk(qn, k_at, v_at, n_sub, ts, u_mat, carry, mask_at):
    acc = None
    for s in reversed(range(n_sub)):
        out, carry = _sb_tile(_dot_nt(qn, k_at(s)), v_at(s), u_mat, carry, mask_at(s))
        acc = out if acc is None else acc + out
    return acc, carry


def _sb_prompt_kernel(q_ref, k_ref, v_ref, g_ref, o_ref, acc_ref, carry_ref, *, scale, valid_from, ts):
    i = pl.program_id(1)
    ta = q_ref.shape[0]
    n_sub = ta // ts
    qn = (q_ref[...] * (-scale)).astype(BF16)
    u_mat = _strict_upper_sum_matrix(ts)
    row = lax.broadcasted_iota(jnp.int32, (ta, ts), 0)
    col = lax.broadcasted_iota(jnp.int32, (ta, ts), 1)

    def chunk(j, mask_at):
        ks = pl.multiple_of(j * ta, ta)
        k_at = lambda s: k_ref[pl.ds(ks + s * ts, ts), :].astype(BF16)
        v_at = lambda s: v_ref[pl.ds(ks + s * ts, ts), :].astype(BF16)
        out, carry = _sb_chunk(qn, k_at, v_at, n_sub, ts, u_mat, carry_ref[...], mask_at)
        acc_ref[...] += out
        carry_ref[...] = carry

    acc_ref[...] = jnp.zeros_like(acc_ref)
    carry_ref[...] = jnp.zeros_like(carry_ref)
    chunk(i, lambda s: (col + s * ts < row) & (col + s * ts + i * ta >= valid_from))

    def body(n, c):
        chunk(i - 1 - n, lambda s: None)
        return c
    lax.fori_loop(0, jnp.maximum(i - 1, 0), body, 0)

    @pl.when(i >= 1)
    def _():
        chunk(0, lambda s: (col + s * ts >= valid_from) if s * ts < valid_from else None)

    o_ref[...] = (acc_ref[...] * _silu(g_ref[...])).astype(o_ref.dtype)


def _sb_prompt(z, c_q, c_k, c_v, c_gate, n_heads, dh, *, rows, valid_from):
    mp = z.shape[0]
    assert rows % TQ == 0 and all(c % dh == 0 for c in (c_q, c_k, c_v, c_gate)) and dh == LANE
    ta = _pick_tile(rows, ATTN_CHUNK, TQ)
    bq, bk, bv, bg = (c // dh for c in (c_q, c_k, c_v, c_gate))
    return pl.pallas_call(
        functools.partial(_sb_prompt_kernel, scale=dh ** -0.5, valid_from=valid_from, ts=TQ),
        grid=(n_heads, rows // ta),
        in_specs=[pl.BlockSpec((ta, dh), lambda h, i: (i, bq + h)),
                  pl.BlockSpec((rows, dh), lambda h, i: (0, bk + h)),
                  pl.BlockSpec((rows, dh), lambda h, i: (0, bv + h)),
                  pl.BlockSpec((ta, dh), lambda h, i: (i, bg + h))],
        out_specs=pl.BlockSpec((ta, dh), lambda h, i: (i, h)),
        out_shape=jax.ShapeDtypeStruct((mp, n_heads * dh), BF16),
        scratch_shapes=[pltpu.VMEM((ta, dh), F32), pltpu.VMEM((ta, LANE), F32)],
        compiler_params=_params(2),
        name="sb_prompt",
    )(z, z, z, z)


def _sb_sample_kernel(q_ref, kn_ref, vn_ref, kc_ref, vc_ref, g_ref, yb_hbm, o_ref, *, scale, tk, n_sub, nh):
    del yb_hbm
    h = pl.program_id(1)
    tq = q_ref.shape[0]
    past = kc_ref.shape[1] // nh
    qn = (q_ref[...] * (-scale)).astype(BF16)
    tn = max(tq, LANE)
    zpad = jnp.zeros((tn - tq, kn_ref.shape[1]), F32)
    kn = jnp.concatenate([kn_ref[...], zpad], axis=0).astype(BF16)
    vn = jnp.concatenate([vn_ref[...], zpad], axis=0).astype(BF16)
    row = lax.broadcasted_iota(jnp.int32, (tq, tn), 0)
    col = lax.broadcasted_iota(jnp.int32, (tq, tn), 1)
    acc, carry = _sb_tile(_dot_nt(qn, kn), vn, _strict_upper_sum_matrix(tn),
                          jnp.zeros((tq, LANE), F32), col < row)
    u_mat = _strict_upper_sum_matrix(tk)
    ck = n_sub * tk

    def body(n, c):
        acc, carry = c
        ks = past - (n + 1) * ck
        rows = lambda s: pl.ds((ks + s * tk) * nh + h, tk, stride=nh)
        k_at = lambda s: kc_ref[0, rows(s), :].astype(BF16)
        v_at = lambda s: vc_ref[0, rows(s), :].astype(BF16)
        out, carry = _sb_chunk(qn, k_at, v_at, n_sub, tk, u_mat, carry, lambda s: None)
        return acc + out, carry
    acc, carry = lax.fori_loop(0, past // ck, body, (acc, carry))
    o_ref[...] = (acc * _silu(g_ref[...])).astype(o_ref.dtype)


def _sb_sample(z, cache_k, cache_v, layer, yb, c_q, c_k, c_v, c_gate, *, row_off, t_new):
    depth, nb, past, n_heads, dh = cache_k.shape
    assert row_off % t_new == 0 and dh == LANE
    tk = _pick_tile(past, TQ, LANE)
    n_sub = _pick_tile(past // tk, 4, 1)
    rb = row_off // t_new
    bq, bk, bv, bg = (c // dh for c in (c_q, c_k, c_v, c_gate))
    row_spec = lambda cb: pl.BlockSpec((t_new, dh), lambda b, h: (rb + b, cb + h))
    cache_spec = pl.BlockSpec((None, 1, past * n_heads, dh), lambda b, h: (layer, b, 0, 0))
    cache_k = cache_k.reshape(depth, nb, past * n_heads, dh)
    cache_v = cache_v.reshape(depth, nb, past * n_heads, dh)
    return pl.pallas_call(
        functools.partial(_sb_sample_kernel, scale=dh ** -0.5, tk=tk, n_sub=n_sub, nh=n_heads),
        grid=(nb, n_heads),
        in_specs=[row_spec(bq), row_spec(bk), row_spec(bv), cache_spec, cache_spec, row_spec(bg),
                  pl.BlockSpec(memory_space=pl.ANY)],
        out_specs=pl.BlockSpec((t_new, dh), lambda b, h: (rb + b, h)),
        out_shape=jax.ShapeDtypeStruct(yb.shape, yb.dtype),
        input_output_aliases={6: 0},
        compiler_params=_params(2),
        name="sb_sample",
    )(z, z, z, cache_k, cache_v, z, yb)


def _q_up_kernel(cq_ref, w_ref, cos_ref, sin_ref, o_ref, *, scale, d_nope):
    cq = cq_ref[...]
    cos = cos_ref[...]
    sin = sin_ref[...]
    d_rope = cos.shape[1]
    for h in range(w_ref.shape[0]):
        r = _dot(cq, w_ref[h])
        rope = r[:, d_nope:d_nope + d_rope] * cos + r[:, d_nope + d_rope:] * sin
        o_ref[h, :, 0:d_nope] = (r[:, :d_nope] * scale).astype(o_ref.dtype)
        o_ref[h, :, d_nope:] = (rope * scale).astype(o_ref.dtype)


def _q_up(cq, wq, cos, sin, *, d_nope, scale):
    m, qr = cq.shape
    nh, _, wcols = wq.shape
    d_rope = cos.shape[1]
    assert wcols == d_nope + 2 * d_rope
    tm = _pick_tile(m, 640, 128)
    return pl.pallas_call(
        functools.partial(_q_up_kernel, scale=scale, d_nope=d_nope),
        grid=(m // tm,),
        in_specs=[pl.BlockSpec((tm, qr), lambda i: (i, 0)),
                  pl.BlockSpec((nh, qr, wcols), lambda i: (0, 0, 0)),
                  pl.BlockSpec((tm, d_rope), lambda i: (i, 0)),
                  pl.BlockSpec((tm, d_rope), lambda i: (i, 0))],
        out_specs=pl.BlockSpec((nh, tm, d_nope + d_rope), lambda i: (0, i, 0)),
        out_shape=jax.ShapeDtypeStruct((nh, m, d_nope + d_rope), BF16),
        compiler_params=_params(1),
        name="q_up",
    )(cq, wq, cos, sin)


def _kv_up_kernel(lat_ref, kr_ref, w_ref, k_ref, v_ref, *, d_nope):
    lat = lat_ref[...].astype(BF16)
    kr = kr_ref[...].astype(BF16)
    for h in range(w_ref.shape[0]):
        r = _dot(lat, w_ref[h])
        k_ref[h, :, 0:d_nope] = r[:, :d_nope].astype(BF16)
        k_ref[h, :, d_nope:] = kr
        v_ref[h] = r[:, d_nope:].astype(BF16)


def _kv_up(lat, kr, wkv, *, rows, d_nope):
    kvr = lat.shape[1]
    d_rope = kr.shape[1]
    nh, _, wcols = wkv.shape
    d_v = wcols - d_nope
    tr = _pick_tile(rows, 768, TQ)
    return pl.pallas_call(
        functools.partial(_kv_up_kernel, d_nope=d_nope),
        grid=(rows // tr,),
        in_specs=[pl.BlockSpec((tr, kvr), lambda i: (i, 0)),
                  pl.BlockSpec((tr, d_rope), lambda i: (i, 0)),
                  pl.BlockSpec((nh, kvr, wcols), lambda i: (0, 0, 0))],
        out_specs=[pl.BlockSpec((nh, tr, d_nope + d_rope), lambda i: (0, i, 0)),
                   pl.BlockSpec((nh, tr, d_v), lambda i: (0, i, 0))],
        out_shape=[jax.ShapeDtypeStruct((nh, rows, d_nope + d_rope), BF16),
                   jax.ShapeDtypeStruct((nh, rows, d_v), BF16)],
        compiler_params=_params(1),
        name="kv_up",
    )(lat, kr, wkv)


def _mla_prompt_kernel(q_ref, k_ref, v_ref, g_ref, o_ref, m_ref, l_ref, acc_ref, *, valid_from):
    i = pl.program_id(1)
    ta = q_ref.shape[1]
    q = q_ref[0]

    def chunk(j, mask):
        ks = pl.multiple_of(j * ta, ta)
        s = _dot_nt(q, k_ref[0, pl.ds(ks, ta), :])
        if mask is not None:
            s = jnp.where(mask, s, NEG_INF)
        m_old = m_ref[...]
        m_new = jnp.maximum(m_old, _row_reduce(s, jnp.maximum, jnp.max))
        alpha = jnp.exp2(m_old - m_new)
        p = jnp.exp2(s - _rep(m_new, ta))
        l_ref[...] = alpha * l_ref[...] + _row_reduce(p, jnp.add, jnp.sum)
        acc_ref[...] = alpha * acc_ref[...] + _dot(p.astype(BF16), v_ref[0, pl.ds(ks, ta), :])
        m_ref[...] = m_new

    m_ref[...] = jnp.full_like(m_ref, NEG_INF)
    l_ref[...] = jnp.zeros_like(l_ref)
    acc_ref[...] = jnp.zeros_like(acc_ref)
    col = lax.broadcasted_iota(jnp.int32, (ta, ta), 1)

    @pl.when(i >= 1)
    def _():
        chunk(0, col >= valid_from)

    def body(j, c):
        chunk(j, None)
        return c
    lax.fori_loop(1, i, body, 0)

    first = TQ // CHUNK - 1
    qc = jnp.maximum((i * ta + lax.broadcasted_iota(jnp.int32, (ta, ta), 0)) // CHUNK - first, 0)
    kc = jnp.maximum((i * ta + col) // CHUNK - first, 0)
    chunk(i, (kc <= qc) & (col + i * ta >= valid_from))

    o_ref[...] = (acc_ref[...] / l_ref[...] * _silu(g_ref[...])).astype(o_ref.dtype)


def _mla_prompt(q_cat, k_cat, v, z, c_gate, *, rows, valid_from):
    nh, mp, dk = q_cat.shape
    dv = v.shape[2]
    assert rows % TQ == 0 and c_gate % dv == 0 and dv == LANE and TQ % CHUNK == 0
    ta = _pick_tile(rows, ATTN_CHUNK, TQ)
    bg = c_gate // dv
    return pl.pallas_call(
        functools.partial(_mla_prompt_kernel, valid_from=valid_from),
        grid=(nh, rows // ta),
        in_specs=[pl.BlockSpec((1, ta, dk), lambda h, i: (h, i, 0)),
                  pl.BlockSpec((1, rows, dk), lambda h, i: (h, 0, 0)),
                  pl.BlockSpec((1, rows, dv), lambda h, i: (h, 0, 0)),
                  pl.BlockSpec((ta, dv), lambda h, i: (i, bg + h))],
        out_specs=pl.BlockSpec((ta, dv), lambda h, i: (i, h)),
        out_shape=jax.ShapeDtypeStruct((mp, nh * dv), BF16),
        scratch_shapes=[pltpu.VMEM((ta, LANE), F32), pltpu.VMEM((ta, LANE), F32), pltpu.VMEM((ta, dv), F32)],
        compiler_params=_params(2),
        name="mla_prompt",
    )(q_cat, k_cat, v, z)


def _mla_sample_kernel(q_ref, latc_ref, krc_ref, latn_ref, krn_ref, w_ref, g_ref, yc_hbm, o_ref,
                       ql_ref, qr_ref, *, d_nope):
    del yc_hbm
    nh, t_new, _ = q_ref.shape
    latc = latc_ref[0].astype(BF16)
    krc = krc_ref[0].astype(BF16)
    latn = latn_ref[...].astype(BF16)
    krn = krn_ref[...].astype(BF16)
    for h in range(nh):
        qh = q_ref[h]
        ql_ref[h * t_new:(h + 1) * t_new, :] = _dot_nt(qh[:, :d_nope], w_ref[h, :, :d_nope]).astype(BF16)
        qr_ref[h * t_new:(h + 1) * t_new, :] = qh[:, d_nope:]
    ql = ql_ref[...]
    qr = qr_ref[...]
    s_c = _dot_nt(ql, latc) + _dot_nt(qr, krc)
    s_n = _dot_nt(ql, latn) + _dot_nt(qr, krn)
    m = jnp.maximum(jnp.max(s_c, axis=1, keepdims=True), jnp.max(s_n, axis=1, keepdims=True))
    p_c = jnp.exp2(s_c - m)
    p_n = jnp.exp2(s_n - m)
    denom = jnp.sum(p_c, axis=1, keepdims=True) + jnp.sum(p_n, axis=1, keepdims=True)
    o_lat = ((_dot(p_c.astype(BF16), latc) + _dot(p_n.astype(BF16), latn)) / denom).astype(BF16)
    d_v = w_ref.shape[2] - d_nope
    for h in range(nh):
        yh = _dot(o_lat[h * t_new:(h + 1) * t_new, :], w_ref[h, :, d_nope:])
        o_ref[:, h * d_v:(h + 1) * d_v] = (yh * _silu(g_ref[:, h * d_v:(h + 1) * d_v])).astype(o_ref.dtype)


def _mla_sample(q_cat, lat_cache, kr_cache, layer, lat_new, kr_new, wkv, z, yc, c_gate, *, row_off, t_new, d_nope):
    nh, mp, dk = q_cat.shape
    _, nb, past, kvr = lat_cache.shape
    d_rope = kr_cache.shape[3]
    d_v = wkv.shape[2] - d_nope
    width = nh * d_v
    assert row_off % t_new == 0 and c_gate % width == 0
    rb = row_off // t_new
    return pl.pallas_call(
        functools.partial(_mla_sample_kernel, d_nope=d_nope),
        grid=(nb,),
        in_specs=[pl.BlockSpec((nh, t_new, dk), lambda b: (0, rb + b, 0)),
                  pl.BlockSpec((None, 1, past, kvr), lambda b: (layer, b, 0, 0)),
                  pl.BlockSpec((None, 1, past, d_rope), lambda b: (layer, b, 0, 0)),
                  pl.BlockSpec((t_new, kvr), lambda b: (rb + b, 0)),
                  pl.BlockSpec((t_new, d_rope), lambda b: (rb + b, 0)),
                  pl.BlockSpec(wkv.shape, lambda b: (0, 0, 0)),
                  pl.BlockSpec((t_new, width), lambda b: (rb + b, c_gate // width)),
                  pl.BlockSpec(memory_space=pl.ANY)],
        out_specs=pl.BlockSpec((t_new, width), lambda b: (rb + b, 0)),
        out_shape=jax.ShapeDtypeStruct(yc.shape, yc.dtype),
        scratch_shapes=[pltpu.VMEM((nh * t_new, kvr), BF16), pltpu.VMEM((nh * t_new, d_rope), BF16)],
        input_output_aliases={7: 0},
        compiler_params=_params(1),
        name="mla_sample",
    )(q_cat, lat_cache, kr_cache, lat_new, kr_new, wkv, z, yc)


def _out_proj_kernel(ya_ref, yb_ref, yc_ref, wa_ref, wb_ref, wc_ref, h_ref, o_ref, *, rc):
    def body(r, carry):
        rs = pl.multiple_of(r * rc, rc)
        sl = pl.ds(rs, rc)
        y = _dot(ya_ref[sl, :], wa_ref[...]) + _dot(yb_ref[sl, :], wb_ref[...]) + _dot(yc_ref[sl, :], wc_ref[...])
        o_ref[sl, :] = h_ref[sl, :] + y
        return carry
    lax.fori_loop(0, ya_ref.shape[0] // rc, body, 0)


def _out_proj(ya, yb, yc, w_out, h):
    m, d = h.shape
    da, db, dc = ya.shape[1], yb.shape[1], yc.shape[1]
    assert da == db and dc % da == 0 and w_out.shape == (da + db + dc, d)
    tm = _pick_tile(m, 1280, 128)
    tn = _pick_tile(d, 512, 128)
    rc = _pick_tile(tm, 640, 128)
    return pl.pallas_call(
        functools.partial(_out_proj_kernel, rc=rc),
        grid=(m // tm, d // tn),
        in_specs=[pl.BlockSpec((tm, da), lambda i, j: (i, 0)),
                  pl.BlockSpec((tm, db), lambda i, j: (i, 0)),
                  pl.BlockSpec((tm, dc), lambda i, j: (i, 0)),
                  pl.BlockSpec((da, tn), lambda i, j: (0, j)),
                  pl.BlockSpec((db, tn), lambda i, j: (1, j)),
                  pl.BlockSpec((dc, tn), lambda i, j: ((da + db) // dc, j)),
                  pl.BlockSpec((tm, tn), lambda i, j: (i, j))],
        out_specs=pl.BlockSpec((tm, tn), lambda i, j: (i, j)),
        out_shape=jax.ShapeDtypeStruct((m, d), F32),
        compiler_params=_params(2),
        name="out_proj",
    )(ya, yb, yc, w_out, w_out, w_out, h)


def _swap_neg_halves(w):
    half = w.shape[-1] // 2
    return jnp.concatenate([-w[..., half:], w[..., :half]], axis=-1)


def kernel(x_prompt, x_sample, cache_a_conv, cache_b_k, cache_b_v, cache_c_latent, cache_c_krope, meta_tokens, norm_g, w_in, a_dw_w, a_dw_b, a_ln_g, a_ln_b, c_q_norm_g, c_w_uq, c_kv_norm_g, c_w_uk, c_w_uv, w_out, final_norm_g):
    bp, seq, d = x_prompt.shape
    nb, t_new, _ = x_sample.shape
    depth = w_in.shape[0]
    n_meta = meta_tokens.shape[0]
    hist, d_a = cache_a_conv.shape[2], cache_a_conv.shape[3]
    past, h_b, dh_b = cache_b_k.shape[2], cache_b_k.shape[3], cache_b_k.shape[4]
    d_b = h_b * dh_b
    kvr = cache_c_latent.shape[3]
    d_rope = cache_c_krope.shape[3]
    qr = c_q_norm_g.shape[1]
    h_c, d_nope = c_w_uk.shape[2], c_w_uk.shape[3]
    d_v = c_w_uv.shape[3]
    d_c = h_c * d_v
    assert bp == 1 and n_meta <= TQ and n_meta % 8 == 0 and seq % TQ == 0
    assert t_new == CHUNK and past % CHUNK == 0 and hist == a_dw_w.shape[1] - 1 and hist <= t_new
    assert d_nope == LANE and d_v == LANE and dh_b == LANE

    pad = TQ - n_meta
    p_end = TQ + seq
    s0 = p_end
    mp = s0 + nb * t_new
    assert (nb * t_new) % LANE == 0

    o_in = {}
    acc = 0
    for name, width in (("a_val", d_a), ("a_glu", d_a), ("a_gate", d_a), ("b_q", d_b), ("b_k", d_b),
                        ("b_v", d_b), ("b_gate", d_b), ("c_q", qr), ("c_kv", kvr), ("c_kr", d_rope),
                        ("c_gate", d_c)):
        o_in[name] = (acc, width)
        acc += width
    assert acc == w_in.shape[2]
    order = ("c_q", "c_kv", "c_gate", "a_val", "a_glu", "a_gate", "b_q", "b_k", "b_v", "b_gate")
    col = {}
    acc = 0
    for name in order:
        col[name] = acc
        acc += o_in[name][1]
    n_main = acc
    assert col["c_q"] % qr == 0 and col["c_kv"] % kvr == 0

    def layer_weights(l):
        def w_cols(name):
            a, wd = o_in[name]
            return w_in[l, :, a:a + wd]
        w_main = jnp.concatenate([w_cols(n).astype(BF16) for n in order], axis=1)
        w_kr = w_cols("c_kr")
        w_kr2 = jnp.concatenate([w_kr, _swap_neg_halves(w_kr)], axis=1).astype(BF16)
        wq4 = c_w_uq[l].reshape(qr, h_c, d_nope + d_rope)
        wq_rope = wq4[..., d_nope:]
        wq = jnp.concatenate([wq4[..., :d_nope], wq_rope, _swap_neg_halves(wq_rope)], axis=-1)
        wq = jnp.transpose(wq, (1, 0, 2)).astype(BF16)
        wkv = jnp.concatenate([jnp.transpose(c_w_uk[l], (1, 0, 2)), jnp.transpose(c_w_uv[l], (1, 0, 2))],
                              axis=-1).astype(BF16)
        return w_main, w_kr2, wq, wkv, w_out[l].astype(BF16)

    pos_prompt = jnp.maximum(jnp.arange(p_end) - pad, 0)
    pos_sample = past + (jnp.arange(nb * t_new) % t_new)
    pos = jnp.concatenate([pos_prompt, pos_sample]).astype(F32)
    half = d_rope // 2
    inv_freq = ROPE_BASE ** (-jnp.arange(half, dtype=F32) / half)
    ang = pos[:, None] * inv_freq[None, :]
    cos = jnp.tile(jnp.cos(ang), (1, 2))
    sin = jnp.tile(jnp.sin(ang), (1, 2))

    h = jnp.concatenate([jnp.zeros((pad, d), F32), meta_tokens.astype(F32), x_prompt[0],
                         x_sample.reshape(nb * t_new, d)], axis=0)
    conv_halo = jnp.pad(cache_a_conv, ((0, 0), (0, 0), (HALO - hist, 0), (0, 0)))
    mla_scale = (d_nope + d_rope) ** -0.5 * LOG2E

    outs = {k: [] for k in ("p_conv", "p_bk", "p_bv", "p_lat", "p_kr", "s_conv", "s_bk", "s_bv", "s_lat", "s_kr")}
    for l in range(depth):
        w_main, w_kr2, wq, wkv, w_out_b = layer_weights(l)
        xn = _rmsnorm(h, norm_g[l], BF16, rows=mp)
        z = _matmul(xn, w_main, F32)
        kr_new = _kr_proj(xn, w_kr2, cos, sin)

        cols_a = (col["a_val"], col["a_glu"], col["a_gate"])
        ya, u = _conv_mixer(z, cols_a, None, a_dw_w[l], a_dw_b[l], a_ln_g[l], a_ln_b[l],
                            tile=TQ, row_off=0, rows=p_end, valid_from=pad)
        ya, u = _conv_mixer(z, cols_a, conv_halo[l], a_dw_w[l], a_dw_b[l], a_ln_g[l], a_ln_b[l],
                            tile=t_new, row_off=s0, rows=nb * t_new, valid_from=0, prev=(ya, u))

        yb = _sb_prompt(z, col["b_q"], col["b_k"], col["b_v"], col["b_gate"], h_b, dh_b,
                        rows=p_end, valid_from=pad)
        yb = _sb_sample(z, cache_b_k, cache_b_v, l, yb, col["b_q"], col["b_k"], col["b_v"],
                        col["b_gate"], row_off=s0, t_new=t_new)

        cq = _rmsnorm(z, c_q_norm_g[l], BF16, rows=mp, col_block=col["c_q"] // qr)
        lat_new = _rmsnorm(z, c_kv_norm_g[l], F32, rows=mp, col_block=col["c_kv"] // kvr)
        q_cat = _q_up(cq, wq, cos, sin, d_nope=d_nope, scale=mla_scale)
        k_cat, v_up = _kv_up(lat_new, kr_new, wkv, rows=p_end, d_nope=d_nope)
        yc = _mla_prompt(q_cat, k_cat, v_up, z, col["c_gate"], rows=p_end, valid_from=pad)
        yc = _mla_sample(q_cat, cache_c_latent, cache_c_krope, l, lat_new, kr_new, wkv, z, yc,
                         col["c_gate"], row_off=s0, t_new=t_new, d_nope=d_nope)

        h = _out_proj(ya, yb, yc, w_out_b, h)

        k_new = z[:, col["b_k"]:col["b_k"] + d_b]
        v_new = z[:, col["b_v"]:col["b_v"] + d_b]
        outs["p_conv"].append(u[p_end - hist:p_end].reshape(1, hist, d_a))
        outs["p_bk"].append(k_new[pad:p_end].reshape(1, n_meta + seq, h_b, dh_b))
        outs["p_bv"].append(v_new[pad:p_end].reshape(1, n_meta + seq, h_b, dh_b))
        outs["p_lat"].append(lat_new[pad:p_end].reshape(1, n_meta + seq, kvr))
        outs["p_kr"].append(kr_new[pad:p_end].reshape(1, n_meta + seq, d_rope))
        outs["s_conv"].append(u[s0:].reshape(nb, t_new, d_a)[:, t_new - hist:])
        outs["s_bk"].append(k_new[s0:].reshape(nb, t_new, h_b, dh_b))
        outs["s_bv"].append(v_new[s0:].reshape(nb, t_new, h_b, dh_b))
        outs["s_lat"].append(lat_new[s0:].reshape(nb, t_new, kvr))
        outs["s_kr"].append(kr_new[s0:].reshape(nb, t_new, d_rope))

    y_prompt = _rmsnorm(h, final_norm_g, F32, rows=seq, row_off=TQ).reshape(1, seq, d)
    y_sample = _rmsnorm(h, final_norm_g, F32, rows=nb * t_new, row_off=s0, tr=t_new).reshape(nb, t_new, d)
    st = lambda k: jnp.stack(outs[k])
    return (y_prompt, y_sample, st("p_conv"), st("p_bk"), st("p_bv"), st("p_lat"), st("p_kr"),
            st("s_conv"), st("s_bk"), st("s_bv"), st("s_lat"), st("s_kr"))
```

```python
import functools

import jax
import jax.numpy as jnp
from jax import lax
from jax.experimental import pallas as pl
from jax.experimental.pallas import tpu as pltpu

CHUNK = 64
EPS = 1e-6
ROPE_BASE = 10000.0
NEG_INF = -1e30
LOG2E = 1.4426950408889634
TQ = 256
ATTN_CHUNK = 768
HALO = 32
LANE = 128
VMEM_LIMIT_BYTES = 56 * 1024 * 1024

F32 = jnp.float32
BF16 = jnp.bfloat16


def _params(n_axes):
    return pltpu.CompilerParams(dimension_semantics=("arbitrary",) * n_axes,
                                vmem_limit_bytes=VMEM_LIMIT_BYTES)


def _pick_tile(n, target, mult):
    best = None
    for t in range(mult, min(n, target) + 1, mult):
        if n % t == 0:
            best = t
    assert best is not None, (n, target, mult)
    return best


def _sigmoid(x):
    return 1.0 / (1.0 + jnp.exp(-x))


def _silu(x):
    return x * _sigmoid(x)


def _dot(a, b):
    return jnp.dot(a, b, preferred_element_type=F32)


def _dot_nt(a, b):
    return lax.dot_general(a, b, (((1,), (1,)), ((), ())), preferred_element_type=F32)


def _row_reduce(x, op, reduce):
    folded = x[:, :LANE]
    for g in range(1, x.shape[1] // LANE):
        folded = op(folded, x[:, g * LANE:(g + 1) * LANE])
    return jnp.broadcast_to(reduce(folded, axis=1, keepdims=True), folded.shape)


def _rep(x, width):
    return x if width == LANE else jnp.concatenate([x] * (width // LANE), axis=1)


def _rmsnorm_kernel(x_ref, g_ref, o_ref):
    x = x_ref[...].astype(F32)
    ms = jnp.mean(x * x, axis=-1, keepdims=True)
    o_ref[...] = (x * lax.rsqrt(ms + EPS) * g_ref[...]).astype(o_ref.dtype)


def _rmsnorm(x, g, out_dtype, *, rows, row_off=0, col_block=0, tr=TQ):
    width = g.shape[-1]
    assert rows % tr == 0 and row_off % tr == 0
    off = row_off // tr
    return pl.pallas_call(
        _rmsnorm_kernel,
        grid=(rows // tr,),
        in_specs=[pl.BlockSpec((tr, width), lambda i: (i + off, col_block)),
                  pl.BlockSpec((1, width), lambda i: (0, 0))],
        out_specs=pl.BlockSpec((tr, width), lambda i: (i, 0)),
        out_shape=jax.ShapeDtypeStruct((rows, width), out_dtype),
        compiler_params=_params(1),
        name="rmsnorm",
    )(x, g.reshape(1, width).astype(F32))


def _w_prep_kernel(w_ref, o_ref, okr_ref, *, segments, kr_src, d_rope):
    for src, dst, width in segments:
        o_ref[:, dst:dst + width] = w_ref[:, src:src + width].astype(BF16)
    half = d_rope // 2
    kr = w_ref[:, kr_src:kr_src + d_rope]
    okr_ref[:, 0:d_rope] = kr.astype(BF16)
    okr_ref[:, d_rope:d_rope + half] = (-kr[:, half:]).astype(BF16)
    okr_ref[:, d_rope + half:] = kr[:, :half].astype(BF16)


def _w_prep(w_in, segments, kr_src, d_rope, n_main):
    depth, k, n_in = w_in.shape
    tk = _pick_tile(k, 256, 16)
    return pl.pallas_call(
        functools.partial(_w_prep_kernel, segments=segments, kr_src=kr_src, d_rope=d_rope),
        grid=(depth, k // tk),
        in_specs=[pl.BlockSpec((None, tk, n_in), lambda l, r: (l, r, 0))],
        out_specs=[pl.BlockSpec((None, tk, n_main), lambda l, r: (l, r, 0)),
                   pl.BlockSpec((None, tk, 2 * d_rope), lambda l, r: (l, r, 0))],
        out_shape=[jax.ShapeDtypeStruct((depth, k, n_main), BF16),
                   jax.ShapeDtypeStruct((depth, k, 2 * d_rope), BF16)],
        compiler_params=_params(2),
        name="w_prep",
    )(w_in)


def _matmul_kernel(x_ref, w_ref, o_ref, *, rc):
    def body(r, carry):
        rs = pl.multiple_of(r * rc, rc)
        o_ref[pl.ds(rs, rc), :] = _dot(x_ref[pl.ds(rs, rc), :], w_ref[...]).astype(o_ref.dtype)
        return carry
    lax.fori_loop(0, x_ref.shape[0] // rc, body, 0)


def _matmul(x, w, layer, out_dtype):
    m, k = x.shape
    n = w.shape[2]
    tm = _pick_tile(m, 1280, 128)
    tn = _pick_tile(n, 1024, 128)
    rc = _pick_tile(tm, 640, 128)
    return pl.pallas_call(
        functools.partial(_matmul_kernel, rc=rc),
        grid=(m // tm, n // tn),
        in_specs=[pl.BlockSpec((tm, k), lambda i, j: (i, 0)),
                  pl.BlockSpec((None, k, tn), lambda i, j: (layer, 0, j))],
        out_specs=pl.BlockSpec((tm, tn), lambda i, j: (i, j)),
        out_shape=jax.ShapeDtypeStruct((m, n), out_dtype),
        compiler_params=_params(2),
        name="in_proj",
    )(x, w)


def _kr_kernel(x_ref, w_ref, cos_ref, sin_ref, o_ref):
    r = _dot(x_ref[...], w_ref[...])
    half = r.shape[1] // 2
    o_ref[...] = r[:, :half] * cos_ref[...] + r[:, half:] * sin_ref[...]


def _kr_proj(xn, w_kr2, layer, cos, sin):
    m, k = xn.shape
    r2 = w_kr2.shape[2]
    tm = _pick_tile(m, 640, 128)
    return pl.pallas_call(
        _kr_kernel,
        grid=(m // tm,),
        in_specs=[pl.BlockSpec((tm, k), lambda i: (i, 0)),
                  pl.BlockSpec((None, k, r2), lambda i: (layer, 0, 0)),
                  pl.BlockSpec((tm, r2 // 2), lambda i: (i, 0)),
                  pl.BlockSpec((tm, r2 // 2), lambda i: (i, 0))],
        out_specs=pl.BlockSpec((tm, r2 // 2), lambda i: (i, 0)),
        out_shape=jax.ShapeDtypeStruct((m, r2 // 2), F32),
        compiler_params=_params(1),
        name="kr_proj",
    )(xn, w_kr2, cos, sin)


def _conv_kernel(val_ref, glu_ref, gate_ref, hval_ref, hglu_ref, w_ref, b_ref, lng_ref, lnb_ref,
                 ya_ref, u_ref, ubuf, cbuf, *, tile, taps, valid_from, halo_from_u):
    t = pl.program_id(0)
    ch = val_ref.shape[1]
    u = val_ref[...] * _sigmoid(glu_ref[...])
    if halo_from_u:
        row = t * tile + lax.broadcasted_iota(jnp.int32, (tile, 1), 0)
        u = jnp.where(row >= valid_from, u, 0.0)
        hu = hval_ref[...] * _sigmoid(hglu_ref[...])
        hrow = t * tile - HALO + lax.broadcasted_iota(jnp.int32, (HALO, 1), 0)
        hu = jnp.where(hrow >= valid_from, hu, 0.0)
    else:
        hu = hval_ref[0]
    u_ref[...] = u
    ubuf[0:HALO, :] = hu
    ubuf[HALO:HALO + tile, :] = u
    first = HALO - (taps - 1)
    for c in range(0, ch, LANE):
        acc = jnp.zeros((tile, LANE), F32) + b_ref[:, c:c + LANE]
        for k in range(taps):
            acc = acc + ubuf[first + k:first + k + tile, c:c + LANE] * w_ref[k:k + 1, c:c + LANE]
        cbuf[:, c:c + LANE] = acc
    y = cbuf[...]
    mu = jnp.mean(y, axis=-1, keepdims=True)
    yc = y - mu
    var = jnp.mean(yc * yc, axis=-1, keepdims=True)
    yn = yc * lax.rsqrt(var + EPS) * lng_ref[...] + lnb_ref[...]
    ya_ref[...] = (_silu(yn) * _silu(gate_ref[...])).astype(ya_ref.dtype)


def _conv_mixer(z, cols, halo, w, b, lng, lnb, *, tile, row_off, rows, valid_from, prev=None):
    mp = z.shape[0]
    ch = w.shape[1]
    taps = w.shape[0]
    assert taps - 1 <= HALO and tile % HALO == 0 and row_off % tile == 0 and rows % tile == 0
    c_val, c_glu, c_gate = (c // ch for c in cols)
    assert all(c % ch == 0 for c in cols)
    off = row_off // tile
    hb = tile // HALO
    halo_from_u = halo is None
    if halo_from_u:
        hspec_v = pl.BlockSpec((HALO, ch), lambda t: (jnp.maximum((t + off) * hb - 1, 0), c_val))
        hspec_g = pl.BlockSpec((HALO, ch), lambda t: (jnp.maximum((t + off) * hb - 1, 0), c_glu))
        hv, hg = z, z
    else:
        hspec_v = pl.BlockSpec((1, HALO, ch), lambda t: (t, 0, 0))
        hspec_g = pl.BlockSpec((1, HALO, ch), lambda t: (t, 0, 0))
        hv, hg = halo, halo
    row_spec = lambda cb: pl.BlockSpec((tile, ch), lambda t: (t + off, cb))
    vec_spec = pl.BlockSpec((1, ch), lambda t: (0, 0))
    out_specs = [pl.BlockSpec((tile, ch), lambda t: (t + off, 0)),
                 pl.BlockSpec((tile, ch), lambda t: (t + off, 0))]
    out_shape = [jax.ShapeDtypeStruct((mp, ch), BF16), jax.ShapeDtypeStruct((mp, ch), F32)]
    in_specs = [row_spec(c_val), row_spec(c_glu), row_spec(c_gate), hspec_v, hspec_g,
                pl.BlockSpec((taps, ch), lambda t: (0, 0)), vec_spec, vec_spec, vec_spec]
    args = [z, z, z, hv, hg, w, b.reshape(1, ch), lng.reshape(1, ch), lnb.reshape(1, ch)]
    aliases = {}
    kernel = functools.partial(_conv_kernel, tile=tile, taps=taps, valid_from=valid_from,
                               halo_from_u=halo_from_u)
    if prev is not None:
        in_specs += [pl.BlockSpec(memory_space=pl.ANY), pl.BlockSpec(memory_space=pl.ANY)]
        args += list(prev)
        aliases = {len(args) - 2: 0, len(args) - 1: 1}
        inner = kernel
        kernel = lambda *refs: inner(*refs[:9], *refs[11:])
    return pl.pallas_call(
        kernel,
        grid=(rows // tile,),
        in_specs=in_specs,
        out_specs=out_specs,
        out_shape=out_shape,
        scratch_shapes=[pltpu.VMEM((HALO + tile, ch), F32), pltpu.VMEM((tile, ch), F32)],
        input_output_aliases=aliases,
        compiler_params=_params(1),
        name="conv_mixer",
    )(*args)


def _strict_upper_sum_matrix(n):
    r = lax.broadcasted_iota(jnp.int32, (n, n), 0)
    c = lax.broadcasted_iota(jnp.int32, (n, n), 1)
    return jnp.where(r > c, 1.0, 0.0).astype(BF16)


_SB_DEAD = -105.0


def _sb_tile(zn, v, u_mat, carry, mask):
    lse = jnp.log(1.0 + jnp.exp(-jnp.abs(zn)))
    log_rest = jnp.minimum(zn, 0.0) - lse
    if mask is not None:
        log_rest = jnp.where(mask, log_rest, 0.0)
    hi = log_rest.astype(BF16)
    lo = (log_rest - hi.astype(F32)).astype(BF16)
    excl = _dot(hi, u_mat) + _dot(lo, u_mat)
    logit = (log_rest - zn) + excl + _rep(carry, zn.shape[1])
    w = jnp.exp(logit)
    if mask is not None:
        w = jnp.where(mask, w, 0.0)
    out = _dot(w.astype(BF16), v)
    return out, carry + _row_reduce(log_rest, jnp.add, jnp.sum)


def _sb_chunk(qn, k_at, v_at, n_sub, ts, u_mat, carry, mask_at):
    acc = None
    for s in reversed(range(n_sub)):
        out, carry = _sb_tile(_dot_nt(qn, k_at(s)), v_at(s), u_mat, carry, mask_at(s))
        acc = out if acc is None else acc + out
    return acc, carry


def _sb_prompt_kernel(q_ref, k_ref, v_ref, g_ref, o_ref, acc_ref, carry_ref, *, scale, valid_from, ts):
    i = pl.program_id(1)
    ta = q_ref.shape[0]
    n_sub = ta // ts
    qn = (q_ref[...] * (-scale)).astype(BF16)
    u_mat = _strict_upper_sum_matrix(ts)
    row = lax.broadcasted_iota(jnp.int32, (ta, ts), 0)
    col = lax.broadcasted_iota(jnp.int32, (ta, ts), 1)

    def chunk(j, mask_at):
        ks = pl.multiple_of(j * ta, ta)
        k_at = lambda s: k_ref[pl.ds(ks + s * ts, ts), :].astype(BF16)
        v_at = lambda s: v_ref[pl.ds(ks + s * ts, ts), :].astype(BF16)
        out, carry = _sb_chunk(qn, k_at, v_at, n_sub, ts, u_mat, carry_ref[...], mask_at)
        acc_ref[...] += out
        carry_ref[...] = carry

    acc_ref[...] = jnp.zeros_like(acc_ref)
    carry_ref[...] = jnp.zeros_like(carry_ref)
    chunk(i, lambda s: (col + s * ts < row) & (col + s * ts + i * ta >= valid_from))

    def live(c):
        n, top = c
        return (n < i - 1) & (top > _SB_DEAD)

    def body(c):
        chunk(i - 1 - c[0], lambda s: None)
        return c[0] + 1, jnp.max(carry_ref[...])
    _, top = lax.while_loop(live, body, (jnp.int32(0), jnp.max(carry_ref[...])))

    @pl.when((i >= 1) & (top > _SB_DEAD))
    def _():
        chunk(0, lambda s: (col + s * ts >= valid_from) if s * ts < valid_from else None)

    o_ref[...] = (acc_ref[...] * _silu(g_ref[...])).astype(o_ref.dtype)


def _sb_prompt(z, c_q, c_k, c_v, c_gate, n_heads, dh, *, rows, valid_from):
    mp = z.shape[0]
    assert rows % TQ == 0 and all(c % dh == 0 for c in (c_q, c_k, c_v, c_gate)) and dh == LANE
    ta = _pick_tile(rows, ATTN_CHUNK, TQ)
    bq, bk, bv, bg = (c // dh for c in (c_q, c_k, c_v, c_gate))
    return pl.pallas_call(
        functools.partial(_sb_prompt_kernel, scale=dh ** -0.5, valid_from=valid_from, ts=TQ),
        grid=(n_heads, rows // ta),
        in_specs=[pl.BlockSpec((ta, dh), lambda h, i: (i, bq + h)),
                  pl.BlockSpec((rows, dh), lambda h, i: (0, bk + h)),
                  pl.BlockSpec((rows, dh), lambda h, i: (0, bv + h)),
                  pl.BlockSpec((ta, dh), lambda h, i: (i, bg + h))],
        out_specs=pl.BlockSpec((ta, dh), lambda h, i: (i, h)),
        out_shape=jax.ShapeDtypeStruct((mp, n_heads * dh), BF16),
        scratch_shapes=[pltpu.VMEM((ta, dh), F32), pltpu.VMEM((ta, LANE), F32)],
        compiler_params=_params(2),
        name="sb_prompt",
    )(z, z, z, z)


def _sb_sample_kernel(q_ref, kn_ref, vn_ref, kc_ref, vc_ref, g_ref, yb_hbm, o_ref, *, scale, tk, n_sub, nh):
    del yb_hbm
    h = pl.program_id(1)
    tq = q_ref.shape[0]
    past = kc_ref.shape[1] // nh
    qn = (q_ref[...] * (-scale)).astype(BF16)
    tn = max(tq, LANE)
    zpad = jnp.zeros((tn - tq, kn_ref.shape[1]), F32)
    kn = jnp.concatenate([kn_ref[...], zpad], axis=0).astype(BF16)
    vn = jnp.concatenate([vn_ref[...], zpad], axis=0).astype(BF16)
    row = lax.broadcasted_iota(jnp.int32, (tq, tn), 0)
    col = lax.broadcasted_iota(jnp.int32, (tq, tn), 1)
    acc, carry = _sb_tile(_dot_nt(qn, kn), vn, _strict_upper_sum_matrix(tn),
                          jnp.zeros((tq, LANE), F32), col < row)
    u_mat = _strict_upper_sum_matrix(tk)
    ck = n_sub * tk

    def live(c):
        return (c[0] < past // ck) & (c[1] > _SB_DEAD)

    def body(c):
        n, _, acc, carry = c
        ks = past - (n + 1) * ck
        rows = lambda s: pl.ds((ks + s * tk) * nh + h, tk, stride=nh)
        k_at = lambda s: kc_ref[0, rows(s), :].astype(BF16)
        v_at = lambda s: vc_ref[0, rows(s), :].astype(BF16)
        out, carry = _sb_chunk(qn, k_at, v_at, n_sub, tk, u_mat, carry, lambda s: None)
        return n + 1, jnp.max(carry), acc + out, carry
    _, _, acc, carry = lax.while_loop(live, body, (jnp.int32(0), jnp.max(carry), acc, carry))
    o_ref[...] = (acc * _silu(g_ref[...])).astype(o_ref.dtype)


def _sb_sample(z, cache_k, cache_v, layer, yb, c_q, c_k, c_v, c_gate, *, row_off, t_new):
    depth, nb, past, n_heads, dh = cache_k.shape
    assert row_off % t_new == 0 and dh == LANE
    tk = _pick_tile(past, TQ, LANE)
    n_sub = _pick_tile(past // tk, 2, 1)
    rb = row_off // t_new
    bq, bk, bv, bg = (c // dh for c in (c_q, c_k, c_v, c_gate))
    row_spec = lambda cb: pl.BlockSpec((t_new, dh), lambda b, h: (rb + b, cb + h))
    cache_spec = pl.BlockSpec((None, 1, past * n_heads, dh), lambda b, h: (layer, b, 0, 0))
    cache_k = cache_k.reshape(depth, nb, past * n_heads, dh)
    cache_v = cache_v.reshape(depth, nb, past * n_heads, dh)
    return pl.pallas_call(
        functools.partial(_sb_sample_kernel, scale=dh ** -0.5, tk=tk, n_sub=n_sub, nh=n_heads),
        grid=(nb, n_heads),
        in_specs=[row_spec(bq), row_spec(bk), row_spec(bv), cache_spec, cache_spec, row_spec(bg),
                  pl.BlockSpec(memory_space=pl.ANY)],
        out_specs=pl.BlockSpec((t_new, dh), lambda b, h: (rb + b, h)),
        out_shape=jax.ShapeDtypeStruct(yb.shape, yb.dtype),
        input_output_aliases={6: 0},
        compiler_params=_params(2),
        name="sb_sample",
    )(z, z, z, cache_k, cache_v, z, yb)


def _q_up_kernel(cq_ref, w_ref, cos_ref, sin_ref, o_ref, *, scale, d_nope):
    cq = cq_ref[...]
    cos = cos_ref[...]
    sin = sin_ref[...]
    d_rope = cos.shape[1]
    for h in range(w_ref.shape[0]):
        r = _dot(cq, w_ref[h])
        rope = r[:, d_nope:d_nope + d_rope] * cos + r[:, d_nope + d_rope:] * sin
        o_ref[h, :, 0:d_nope] = (r[:, :d_nope] * scale).astype(o_ref.dtype)
        o_ref[h, :, d_nope:] = (rope * scale).astype(o_ref.dtype)


def _q_up(cq, wq, cos, sin, *, d_nope, scale):
    m, qr = cq.shape
    nh, _, wcols = wq.shape
    d_rope = cos.shape[1]
    assert wcols == d_nope + 2 * d_rope
    tm = _pick_tile(m, 640, 128)
    return pl.pallas_call(
        functools.partial(_q_up_kernel, scale=scale, d_nope=d_nope),
        grid=(m // tm,),
        in_specs=[pl.BlockSpec((tm, qr), lambda i: (i, 0)),
                  pl.BlockSpec((nh, qr, wcols), lambda i: (0, 0, 0)),
                  pl.BlockSpec((tm, d_rope), lambda i: (i, 0)),
                  pl.BlockSpec((tm, d_rope), lambda i: (i, 0))],
        out_specs=pl.BlockSpec((nh, tm, d_nope + d_rope), lambda i: (0, i, 0)),
        out_shape=jax.ShapeDtypeStruct((nh, m, d_nope + d_rope), BF16),
        compiler_params=_params(1),
        name="q_up",
    )(cq, wq, cos, sin)


def _kv_up_kernel(lat_ref, kr_ref, w_ref, k_ref, v_ref, *, d_nope):
    lat = lat_ref[...].astype(BF16)
    kr = kr_ref[...].astype(BF16)
    for h in range(w_ref.shape[0]):
        r = _dot(lat, w_ref[h])
        k_ref[h, :, 0:d_nope] = r[:, :d_nope].astype(BF16)
        k_ref[h, :, d_nope:] = kr
        v_ref[h] = r[:, d_nope:].astype(BF16)


def _kv_up(lat, kr, wkv, *, rows, d_nope):
    kvr = lat.shape[1]
    d_rope = kr.shape[1]
    nh, _, wcols = wkv.shape
    d_v = wcols - d_nope
    tr = _pick_tile(rows, 768, TQ)
    return pl.pallas_call(
        functools.partial(_kv_up_kernel, d_nope=d_nope),
        grid=(rows // tr,),
        in_specs=[pl.BlockSpec((tr, kvr), lambda i: (i, 0)),
                  pl.BlockSpec((tr, d_rope), lambda i: (i, 0)),
                  pl.BlockSpec((nh, kvr, wcols), lambda i: (0, 0, 0))],
        out_specs=[pl.BlockSpec((nh, tr, d_nope + d_rope), lambda i: (0, i, 0)),
                   pl.BlockSpec((nh, tr, d_v), lambda i: (0, i, 0))],
        out_shape=[jax.ShapeDtypeStruct((nh, rows, d_nope + d_rope), BF16),
                   jax.ShapeDtypeStruct((nh, rows, d_v), BF16)],
        compiler_params=_params(1),
        name="kv_up",
    )(lat, kr, wkv)


def _mla_prompt_kernel(q_ref, k_ref, v_ref, g_ref, o_ref, m_ref, l_ref, acc_ref, *, valid_from):
    i = pl.program_id(1)
    hg, ta, _ = q_ref.shape
    dv = v_ref.shape[2]

    def chunk(j, mask):
        ks = pl.multiple_of(j * ta, ta)
        for hh in range(hg):
            s = _dot_nt(q_ref[hh], k_ref[hh, pl.ds(ks, ta), :])
            if mask is not None:
                s = jnp.where(mask, s, NEG_INF)
            m_old = m_ref[hh]
            m_new = jnp.maximum(m_old, _row_reduce(s, jnp.maximum, jnp.max))
            alpha = jnp.exp2(m_old - m_new)
            p = jnp.exp2(s - _rep(m_new, ta))
            l_ref[hh] = alpha * l_ref[hh] + _row_reduce(p, jnp.add, jnp.sum)
            acc_ref[hh] = alpha * acc_ref[hh] + _dot(p.astype(BF16), v_ref[hh, pl.ds(ks, ta), :])
            m_ref[hh] = m_new

    m_ref[...] = jnp.full_like(m_ref, NEG_INF)
    l_ref[...] = jnp.zeros_like(l_ref)
    acc_ref[...] = jnp.zeros_like(acc_ref)
    col = lax.broadcasted_iota(jnp.int32, (ta, ta), 1)

    @pl.when(i >= 1)
    def _():
        chunk(0, col >= valid_from)

    def body(j, c):
        chunk(j, None)
        return c
    lax.fori_loop(1, i, body, 0)

    first = TQ // CHUNK - 1
    qc = jnp.maximum((i * ta + lax.broadcasted_iota(jnp.int32, (ta, ta), 0)) // CHUNK - first, 0)
    kc = jnp.maximum((i * ta + col) // CHUNK - first, 0)
    chunk(i, (kc <= qc) & (col + i * ta >= valid_from))

    for hh in range(hg):
        cols = slice(hh * dv, (hh + 1) * dv)
        o_ref[:, cols] = (acc_ref[hh] / l_ref[hh] * _silu(g_ref[:, cols])).astype(o_ref.dtype)


def _mla_prompt(q_cat, k_cat, v, z, c_gate, *, rows, valid_from):
    nh, mp, dk = q_cat.shape
    dv = v.shape[2]
    hg = 2 if nh % 2 == 0 else 1
    assert rows % TQ == 0 and c_gate % (hg * dv) == 0 and dv == LANE and TQ % CHUNK == 0
    ta = _pick_tile(rows, ATTN_CHUNK, TQ)
    bg = c_gate // (hg * dv)
    return pl.pallas_call(
        functools.partial(_mla_prompt_kernel, valid_from=valid_from),
        grid=(nh // hg, rows // ta),
        in_specs=[pl.BlockSpec((hg, ta, dk), lambda h, i: (h, i, 0)),
                  pl.BlockSpec((hg, rows, dk), lambda h, i: (h, 0, 0)),
                  pl.BlockSpec((hg, rows, dv), lambda h, i: (h, 0, 0)),
                  pl.BlockSpec((ta, hg * dv), lambda h, i: (i, bg + h))],
        out_specs=pl.BlockSpec((ta, hg * dv), lambda h, i: (i, h)),
        out_shape=jax.ShapeDtypeStruct((mp, nh * dv), BF16),
        scratch_shapes=[pltpu.VMEM((hg, ta, LANE), F32), pltpu.VMEM((hg, ta, LANE), F32),
                        pltpu.VMEM((hg, ta, dv), F32)],
        compiler_params=_params(2),
        name="mla_prompt",
    )(q_cat, k_cat, v, z)


def _mla_sample_kernel(q_ref, latc_ref, krc_ref, latn_ref, krn_ref, w_ref, g_ref, yc_hbm, o_ref,
                       ql_ref, qr_ref, *, d_nope):
    del yc_hbm
    nh, t_new, _ = q_ref.shape
    latc = latc_ref[0].astype(BF16)
    krc = krc_ref[0].astype(BF16)
    latn = latn_ref[...].astype(BF16)
    krn = krn_ref[...].astype(BF16)
    for h in range(nh):
        qh = q_ref[h]
        ql_ref[h * t_new:(h + 1) * t_new, :] = _dot_nt(qh[:, :d_nope], w_ref[h, :, :d_nope]).astype(BF16)
        qr_ref[h * t_new:(h + 1) * t_new, :] = qh[:, d_nope:]
    ql = ql_ref[...]
    qr = qr_ref[...]
    s_c = _dot_nt(ql, latc) + _dot_nt(qr, krc)
    s_n = _dot_nt(ql, latn) + _dot_nt(qr, krn)
    m = jnp.maximum(jnp.max(s_c, axis=1, keepdims=True), jnp.max(s_n, axis=1, keepdims=True))
    p_c = jnp.exp2(s_c - m)
    p_n = jnp.exp2(s_n - m)
    denom = jnp.sum(p_c, axis=1, keepdims=True) + jnp.sum(p_n, axis=1, keepdims=True)
    o_lat = ((_dot(p_c.astype(BF16), latc) + _dot(p_n.astype(BF16), latn)) / denom).astype(BF16)
    d_v = w_ref.shape[2] - d_nope
    for h in range(nh):
        yh = _dot(o_lat[h * t_new:(h + 1) * t_new, :], w_ref[h, :, d_nope:])
        o_ref[:, h * d_v:(h + 1) * d_v] = (yh * _silu(g_ref[:, h * d_v:(h + 1) * d_v])).astype(o_ref.dtype)


def _mla_sample(q_cat, lat_cache, kr_cache, layer, lat_new, kr_new, wkv, z, yc, c_gate, *, row_off, t_new, d_nope):
    nh, mp, dk = q_cat.shape
    _, nb, past, kvr = lat_cache.shape
    d_rope = kr_cache.shape[3]
    d_v = wkv.shape[2] - d_nope
    width = nh * d_v
    assert row_off % t_new == 0 and c_gate % width == 0
    rb = row_off // t_new
    return pl.pallas_call(
        functools.partial(_mla_sample_kernel, d_nope=d_nope),
        grid=(nb,),
        in_specs=[pl.BlockSpec((nh, t_new, dk), lambda b: (0, rb + b, 0)),
                  pl.BlockSpec((None, 1, past, kvr), lambda b: (layer, b, 0, 0)),
                  pl.BlockSpec((None, 1, past, d_rope), lambda b: (layer, b, 0, 0)),
                  pl.BlockSpec((t_new, kvr), lambda b: (rb + b, 0)),
                  pl.BlockSpec((t_new, d_rope), lambda b: (rb + b, 0)),
                  pl.BlockSpec(wkv.shape, lambda b: (0, 0, 0)),
                  pl.BlockSpec((t_new, width), lambda b: (rb + b, c_gate // width)),
                  pl.BlockSpec(memory_space=pl.ANY)],
        out_specs=pl.BlockSpec((t_new, width), lambda b: (rb + b, 0)),
        out_shape=jax.ShapeDtypeStruct(yc.shape, yc.dtype),
        scratch_shapes=[pltpu.VMEM((nh * t_new, kvr), BF16), pltpu.VMEM((nh * t_new, d_rope), BF16)],
        input_output_aliases={7: 0},
        compiler_params=_params(1),
        name="mla_sample",
    )(q_cat, lat_cache, kr_cache, lat_new, kr_new, wkv, z, yc)


def _out_proj_kernel(ya_ref, yb_ref, yc_ref, wa_ref, wb_ref, wc_ref, h_ref, o_ref, *, rc):
    def body(r, carry):
        rs = pl.multiple_of(r * rc, rc)
        sl = pl.ds(rs, rc)
        y = _dot(ya_ref[sl, :], wa_ref[...]) + _dot(yb_ref[sl, :], wb_ref[...]) + _dot(yc_ref[sl, :], wc_ref[...])
        o_ref[sl, :] = h_ref[sl, :] + y
        return carry
    lax.fori_loop(0, ya_ref.shape[0] // rc, body, 0)


def _out_proj(ya, yb, yc, w_out, h):
    m, d = h.shape
    da, db, dc = ya.shape[1], yb.shape[1], yc.shape[1]
    assert da == db and dc % da == 0 and w_out.shape == (da + db + dc, d)
    tm = _pick_tile(m, 1280, 128)
    tn = _pick_tile(d, 512, 128)
    rc = _pick_tile(tm, 640, 128)
    return pl.pallas_call(
        functools.partial(_out_proj_kernel, rc=rc),
        grid=(m // tm, d // tn),
        in_specs=[pl.BlockSpec((tm, da), lambda i, j: (i, 0)),
                  pl.BlockSpec((tm, db), lambda i, j: (i, 0)),
                  pl.BlockSpec((tm, dc), lambda i, j: (i, 0)),
                  pl.BlockSpec((da, tn), lambda i, j: (0, j)),
                  pl.BlockSpec((db, tn), lambda i, j: (1, j)),
                  pl.BlockSpec((dc, tn), lambda i, j: ((da + db) // dc, j)),
                  pl.BlockSpec((tm, tn), lambda i, j: (i, j))],
        out_specs=pl.BlockSpec((tm, tn), lambda i, j: (i, j)),
        out_shape=jax.ShapeDtypeStruct((m, d), F32),
        compiler_params=_params(2),
        name="out_proj",
    )(ya, yb, yc, w_out, w_out, w_out, h)


def _swap_neg_halves(w):
    half = w.shape[-1] // 2
    return jnp.concatenate([-w[..., half:], w[..., :half]], axis=-1)


def kernel(x_prompt, x_sample, cache_a_conv, cache_b_k, cache_b_v, cache_c_latent, cache_c_krope, meta_tokens, norm_g, w_in, a_dw_w, a_dw_b, a_ln_g, a_ln_b, c_q_norm_g, c_w_uq, c_kv_norm_g, c_w_uk, c_w_uv, w_out, final_norm_g):
    bp, seq, d = x_prompt.shape
    nb, t_new, _ = x_sample.shape
    depth = w_in.shape[0]
    n_meta = meta_tokens.shape[0]
    hist, d_a = cache_a_conv.shape[2], cache_a_conv.shape[3]
    past, h_b, dh_b = cache_b_k.shape[2], cache_b_k.shape[3], cache_b_k.shape[4]
    d_b = h_b * dh_b
    kvr = cache_c_latent.shape[3]
    d_rope = cache_c_krope.shape[3]
    qr = c_q_norm_g.shape[1]
    h_c, d_nope = c_w_uk.shape[2], c_w_uk.shape[3]
    d_v = c_w_uv.shape[3]
    d_c = h_c * d_v
    assert bp == 1 and n_meta <= TQ and n_meta % 8 == 0 and seq % TQ == 0
    assert t_new == CHUNK and past % CHUNK == 0 and hist == a_dw_w.shape[1] - 1 and hist <= t_new
    assert d_nope == LANE and d_v == LANE and dh_b == LANE

    pad = TQ - n_meta
    p_end = TQ + seq
    s0 = p_end
    mp = s0 + nb * t_new
    assert (nb * t_new) % LANE == 0

    o_in = {}
    acc = 0
    for name, width in (("a_val", d_a), ("a_glu", d_a), ("a_gate", d_a), ("b_q", d_b), ("b_k", d_b),
                        ("b_v", d_b), ("b_gate", d_b), ("c_q", qr), ("c_kv", kvr), ("c_kr", d_rope),
                        ("c_gate", d_c)):
        o_in[name] = (acc, width)
        acc += width
    assert acc == w_in.shape[2]
    order = ("c_q", "c_kv", "c_gate", "a_val", "a_glu", "a_gate", "b_q", "b_k", "b_v", "b_gate")
    col = {}
    acc = 0
    for name in order:
        col[name] = acc
        acc += o_in[name][1]
    n_main = acc
    assert col["c_q"] % qr == 0 and col["c_kv"] % kvr == 0

    segments = tuple((o_in[n][0], col[n], o_in[n][1]) for n in order)
    w_main, w_kr2 = _w_prep(w_in, segments, o_in["c_kr"][0], d_rope, n_main)

    def layer_weights(l):
        wq4 = c_w_uq[l].reshape(qr, h_c, d_nope + d_rope)
        wq_rope = wq4[..., d_nope:]
        wq = jnp.concatenate([wq4[..., :d_nope], wq_rope, _swap_neg_halves(wq_rope)], axis=-1)
        wq = jnp.transpose(wq, (1, 0, 2)).astype(BF16)
        wkv = jnp.concatenate([jnp.transpose(c_w_uk[l], (1, 0, 2)), jnp.transpose(c_w_uv[l], (1, 0, 2))],
                              axis=-1).astype(BF16)
        return wq, wkv, w_out[l].astype(BF16)

    pos_prompt = jnp.maximum(jnp.arange(p_end) - pad, 0)
    pos_sample = past + (jnp.arange(nb * t_new) % t_new)
    pos = jnp.concatenate([pos_prompt, pos_sample]).astype(F32)
    half = d_rope // 2
    inv_freq = ROPE_BASE ** (-jnp.arange(half, dtype=F32) / half)
    ang = pos[:, None] * inv_freq[None, :]
    cos = jnp.tile(jnp.cos(ang), (1, 2))
    sin = jnp.tile(jnp.sin(ang), (1, 2))

    h = jnp.concatenate([jnp.zeros((pad, d), F32), meta_tokens.astype(F32), x_prompt[0],
                         x_sample.reshape(nb * t_new, d)], axis=0)
    conv_halo = jnp.pad(cache_a_conv, ((0, 0), (0, 0), (HALO - hist, 0), (0, 0)))
    mla_scale = (d_nope + d_rope) ** -0.5 * LOG2E

    outs = {k: [] for k in ("p_conv", "p_bk", "p_bv", "p_lat", "p_kr", "s_conv", "s_bk", "s_bv", "s_lat", "s_kr")}
    for l in range(depth):
        wq, wkv, w_out_b = layer_weights(l)
        xn = _rmsnorm(h, norm_g[l], BF16, rows=mp)
        z = _matmul(xn, w_main, l, F32)
        kr_new = _kr_proj(xn, w_kr2, l, cos, sin)

        cols_a = (col["a_val"], col["a_glu"], col["a_gate"])
        ya, u = _conv_mixer(z, cols_a, None, a_dw_w[l], a_dw_b[l], a_ln_g[l], a_ln_b[l],
                            tile=TQ, row_off=0, rows=p_end, valid_from=pad)
        ya, u = _conv_mixer(z, cols_a, conv_halo[l], a_dw_w[l], a_dw_b[l], a_ln_g[l], a_ln_b[l],
                            tile=t_new, row_off=s0, rows=nb * t_new, valid_from=0, prev=(ya, u))

        yb = _sb_prompt(z, col["b_q"], col["b_k"], col["b_v"], col["b_gate"], h_b, dh_b,
                        rows=p_end, valid_from=pad)
        yb = _sb_sample(z, cache_b_k, cache_b_v, l, yb, col["b_q"], col["b_k"], col["b_v"],
                        col["b_gate"], row_off=s0, t_new=t_new)

        cq = _rmsnorm(z, c_q_norm_g[l], BF16, rows=mp, col_block=col["c_q"] // qr)
        lat_new = _rmsnorm(z, c_kv_norm_g[l], F32, rows=mp, col_block=col["c_kv"] // kvr)
        q_cat = _q_up(cq, wq, cos, sin, d_nope=d_nope, scale=mla_scale)
        k_cat, v_up = _kv_up(lat_new, kr_new, wkv, rows=p_end, d_nope=d_nope)
        yc = _mla_prompt(q_cat, k_cat, v_up, z, col["c_gate"], rows=p_end, valid_from=pad)
        yc = _mla_sample(q_cat, cache_c_latent, cache_c_krope, l, lat_new, kr_new, wkv, z, yc,
                         col["c_gate"], row_off=s0, t_new=t_new, d_nope=d_nope)

        h = _out_proj(ya, yb, yc, w_out_b, h)

        k_new = z[:, col["b_k"]:col["b_k"] + d_b]
        v_new = z[:, col["b_v"]:col["b_v"] + d_b]
        outs["p_conv"].append(u[p_end - hist:p_end].reshape(1, hist, d_a))
        outs["p_bk"].append(k_new[pad:p_end].reshape(1, n_meta + seq, h_b, dh_b))
        outs["p_bv"].append(v_new[pad:p_end].reshape(1, n_meta + seq, h_b, dh_b))
        outs["p_lat"].append(lat_new[pad:p_end].reshape(1, n_meta + seq, kvr))
        outs["p_kr"].append(kr_new[pad:p_end].reshape(1, n_meta + seq, d_rope))
        outs["s_conv"].append(u[s0:].reshape(nb, t_new, d_a)[:, t_new - hist:])
        outs["s_bk"].append(k_new[s0:].reshape(nb, t_new, h_b, dh_b))
        outs["s_bv"].append(v_new[s0:].reshape(nb, t_new, h_b, dh_b))
        outs["s_lat"].append(lat_new[s0:].reshape(nb, t_new, kvr))
        outs["s_kr"].append(kr_new[s0:].reshape(nb, t_new, d_rope))

    y_prompt = _rmsnorm(h, final_norm_g, F32, rows=seq, row_off=TQ).reshape(1, seq, d)
    y_sample = _rmsnorm(h, final_norm_g, F32, rows=nb * t_new, row_off=s0, tr=t_new).reshape(nb, t_new, d)
    st = lambda k: jnp.stack(outs[k])
    return (y_prompt, y_sample, st("p_conv"), st("p_bk"), st("p_bv"), st("p_lat"), st("p_kr"),
            st("s_conv"), st("s_bk"), st("s_bv"), st("s_lat"), st("s_kr"))
```

```python
import functools

import jax
import jax.numpy as jnp
from jax import lax
from jax.experimental import pallas as pl
from jax.experimental.pallas import tpu as pltpu

CHUNK = 64
EPS = 1e-6
ROPE_BASE = 10000.0
NEG_INF = -1e30
LOG2E = 1.4426950408889634
TQ = 256
ATTN_CHUNK = 768
HALO = 32
LANE = 128
SUBLANE = 8
VMEM_LIMIT_BYTES = 56 * 1024 * 1024

F32 = jnp.float32
BF16 = jnp.bfloat16


def _params(n_axes):
    return pltpu.CompilerParams(dimension_semantics=("arbitrary",) * n_axes,
                                vmem_limit_bytes=VMEM_LIMIT_BYTES)


def _pick_tile(n, target, mult):
    best = None
    for t in range(mult, min(n, target) + 1, mult):
        if n % t == 0:
            best = t
    assert best is not None, (n, target, mult)
    return best


def _sigmoid(x):
    return 1.0 / (1.0 + jnp.exp(-x))


def _silu(x):
    return x * _sigmoid(x)


def _dot(a, b):
    return jnp.dot(a, b, preferred_element_type=F32)


def _dot_nt(a, b):
    return lax.dot_general(a, b, (((1,), (1,)), ((), ())), preferred_element_type=F32)


def _row_reduce(x, op, reduce):
    folded = x[:, :LANE]
    for g in range(1, x.shape[1] // LANE):
        folded = op(folded, x[:, g * LANE:(g + 1) * LANE])
    return jnp.broadcast_to(reduce(folded, axis=1, keepdims=True), folded.shape)


def _rep(x, width):
    return x if width == LANE else jnp.concatenate([x] * (width // LANE), axis=1)


def _rmsnorm_kernel(x_ref, g_ref, o_ref):
    x = x_ref[...].astype(F32)
    ms = jnp.mean(x * x, axis=-1, keepdims=True)
    o_ref[...] = (x * lax.rsqrt(ms + EPS) * g_ref[...]).astype(o_ref.dtype)


def _rmsnorm(x, g, out_dtype, *, rows, row_off=0, col_block=0, tr=TQ):
    width = g.shape[-1]
    assert rows % tr == 0 and row_off % tr == 0
    off = row_off // tr
    return pl.pallas_call(
        _rmsnorm_kernel,
        grid=(rows // tr,),
        in_specs=[pl.BlockSpec((tr, width), lambda i: (i + off, col_block)),
                  pl.BlockSpec((1, width), lambda i: (0, 0))],
        out_specs=pl.BlockSpec((tr, width), lambda i: (i, 0)),
        out_shape=jax.ShapeDtypeStruct((rows, width), out_dtype),
        compiler_params=_params(1),
        name="rmsnorm",
    )(x, g.reshape(1, width).astype(F32))


def _w_prep_kernel(w_ref, o_ref, okr_ref, *, segments, kr_src, d_rope):
    for src, dst, width in segments:
        o_ref[dst:dst + width, :] = w_ref[src:src + width, :].astype(BF16)
    half = d_rope // 2
    okr_ref[0:d_rope, :] = w_ref[kr_src:kr_src + d_rope, :].astype(BF16)
    okr_ref[d_rope:d_rope + half, :] = (-w_ref[kr_src + half:kr_src + d_rope, :]).astype(BF16)
    okr_ref[d_rope + half:, :] = w_ref[kr_src:kr_src + half, :].astype(BF16)


def _w_prep(w_in_t, segments, kr_src, d_rope, n_main):
    depth, n_in, k = w_in_t.shape
    tk = _pick_tile(k, 256, LANE)
    return pl.pallas_call(
        functools.partial(_w_prep_kernel, segments=segments, kr_src=kr_src, d_rope=d_rope),
        grid=(depth, k // tk),
        in_specs=[pl.BlockSpec((None, n_in, tk), lambda l, r: (l, 0, r))],
        out_specs=[pl.BlockSpec((None, n_main, tk), lambda l, r: (l, 0, r)),
                   pl.BlockSpec((None, 2 * d_rope, tk), lambda l, r: (l, 0, r))],
        out_shape=[jax.ShapeDtypeStruct((depth, n_main, k), BF16),
                   jax.ShapeDtypeStruct((depth, 2 * d_rope, k), BF16)],
        compiler_params=_params(2),
        name="w_prep",
    )(w_in_t)


def _matmul_kernel(x_ref, w_ref, o_ref, *, rc):
    def body(r, carry):
        rs = pl.multiple_of(r * rc, rc)
        o_ref[pl.ds(rs, rc), :] = _dot_nt(x_ref[pl.ds(rs, rc), :], w_ref[...]).astype(o_ref.dtype)
        return carry
    lax.fori_loop(0, x_ref.shape[0] // rc, body, 0)


def _matmul(x, w_t, layer, out_dtype):
    m, k = x.shape
    n = w_t.shape[1]
    tm = _pick_tile(m, 1280, 128)
    tn = _pick_tile(n, 1024, 128)
    rc = _pick_tile(tm, 640, 128)
    return pl.pallas_call(
        functools.partial(_matmul_kernel, rc=rc),
        grid=(m // tm, n // tn),
        in_specs=[pl.BlockSpec((tm, k), lambda i, j: (i, 0)),
                  pl.BlockSpec((None, tn, k), lambda i, j: (layer, j, 0))],
        out_specs=pl.BlockSpec((tm, tn), lambda i, j: (i, j)),
        out_shape=jax.ShapeDtypeStruct((m, n), out_dtype),
        compiler_params=_params(2),
        name="in_proj",
    )(x, w_t)


def _kr_kernel(x_ref, w_ref, cos_ref, sin_ref, o_ref):
    r = _dot_nt(x_ref[...], w_ref[...])
    half = r.shape[1] // 2
    o_ref[...] = r[:, :half] * cos_ref[...] + r[:, half:] * sin_ref[...]


def _kr_proj(xn, w_kr2, layer, cos, sin):
    m, k = xn.shape
    r2 = w_kr2.shape[1]
    tm = _pick_tile(m, 640, 128)
    return pl.pallas_call(
        _kr_kernel,
        grid=(m // tm,),
        in_specs=[pl.BlockSpec((tm, k), lambda i: (i, 0)),
                  pl.BlockSpec((None, r2, k), lambda i: (layer, 0, 0)),
                  pl.BlockSpec((tm, r2 // 2), lambda i: (i, 0)),
                  pl.BlockSpec((tm, r2 // 2), lambda i: (i, 0))],
        out_specs=pl.BlockSpec((tm, r2 // 2), lambda i: (i, 0)),
        out_shape=jax.ShapeDtypeStruct((m, r2 // 2), F32),
        compiler_params=_params(1),
        name="kr_proj",
    )(xn, w_kr2, cos, sin)


def _conv_kernel(val_ref, glu_ref, gate_ref, hval_ref, hglu_ref, w_ref, b_ref, lng_ref, lnb_ref,
                 ya_ref, u_ref, ubuf, sbuf, cbuf, *, tile, taps, valid_from, halo_from_u):
    t = pl.program_id(0)
    ch = val_ref.shape[1]
    u = val_ref[...] * _sigmoid(glu_ref[...])
    if halo_from_u:
        row = t * tile + lax.broadcasted_iota(jnp.int32, (tile, 1), 0)
        u = jnp.where(row >= valid_from, u, 0.0)
        hu = hval_ref[...] * _sigmoid(hglu_ref[...])
        hrow = t * tile - HALO + lax.broadcasted_iota(jnp.int32, (HALO, 1), 0)
        hu = jnp.where(hrow >= valid_from, hu, 0.0)
    else:
        hu = hval_ref[0]
    u_ref[...] = u
    ubuf[0:HALO, :] = hu
    ubuf[HALO:HALO + tile, :] = u
    span = HALO + tile - SUBLANE
    for r in range(1, SUBLANE):
        sbuf[r - 1] = ubuf[r:r + span, :]
    first = HALO - (taps - 1)
    for c in range(0, ch, LANE):
        acc = jnp.zeros((tile, LANE), F32) + b_ref[:, c:c + LANE]
        for k in range(taps):
            r = (first + k) % SUBLANE
            base = first + k - r
            if r == 0:
                rows = ubuf[base:base + tile, c:c + LANE]
            else:
                rows = sbuf[r - 1, base:base + tile, c:c + LANE]
            acc = acc + rows * w_ref[k:k + 1, c:c + LANE]
        cbuf[:, c:c + LANE] = acc
    y = cbuf[...]
    mu = jnp.mean(y, axis=-1, keepdims=True)
    yc = y - mu
    var = jnp.mean(yc * yc, axis=-1, keepdims=True)
    yn = yc * lax.rsqrt(var + EPS) * lng_ref[...] + lnb_ref[...]
    ya_ref[...] = (_silu(yn) * _silu(gate_ref[...])).astype(ya_ref.dtype)


def _conv_mixer(z, cols, halo, w, b, lng, lnb, *, tile, row_off, rows, valid_from, prev=None):
    mp = z.shape[0]
    ch = w.shape[1]
    taps = w.shape[0]
    assert taps - 1 <= HALO and tile % HALO == 0 and row_off % tile == 0 and rows % tile == 0
    c_val, c_glu, c_gate = (c // ch for c in cols)
    assert all(c % ch == 0 for c in cols)
    off = row_off // tile
    hb = tile // HALO
    halo_from_u = halo is None
    if halo_from_u:
        hspec_v = pl.BlockSpec((HALO, ch), lambda t: (jnp.maximum((t + off) * hb - 1, 0), c_val))
        hspec_g = pl.BlockSpec((HALO, ch), lambda t: (jnp.maximum((t + off) * hb - 1, 0), c_glu))
        hv, hg = z, z
    else:
        hspec_v = pl.BlockSpec((1, HALO, ch), lambda t: (t, 0, 0))
        hspec_g = pl.BlockSpec((1, HALO, ch), lambda t: (t, 0, 0))
        hv, hg = halo, halo
    row_spec = lambda cb: pl.BlockSpec((tile, ch), lambda t: (t + off, cb))
    vec_spec = pl.BlockSpec((1, ch), lambda t: (0, 0))
    out_specs = [pl.BlockSpec((tile, ch), lambda t: (t + off, 0)),
                 pl.BlockSpec((tile, ch), lambda t: (t + off, 0))]
    out_shape = [jax.ShapeDtypeStruct((mp, ch), BF16), jax.ShapeDtypeStruct((mp, ch), F32)]
    in_specs = [row_spec(c_val), row_spec(c_glu), row_spec(c_gate), hspec_v, hspec_g,
                pl.BlockSpec((taps, ch), lambda t: (0, 0)), vec_spec, vec_spec, vec_spec]
    args = [z, z, z, hv, hg, w, b.reshape(1, ch), lng.reshape(1, ch), lnb.reshape(1, ch)]
    aliases = {}
    kernel = functools.partial(_conv_kernel, tile=tile, taps=taps, valid_from=valid_from,
                               halo_from_u=halo_from_u)
    if prev is not None:
        in_specs += [pl.BlockSpec(memory_space=pl.ANY), pl.BlockSpec(memory_space=pl.ANY)]
        args += list(prev)
        aliases = {len(args) - 2: 0, len(args) - 1: 1}
        inner = kernel
        kernel = lambda *refs: inner(*refs[:9], *refs[11:])
    return pl.pallas_call(
        kernel,
        grid=(rows // tile,),
        in_specs=in_specs,
        out_specs=out_specs,
        out_shape=out_shape,
        scratch_shapes=[pltpu.VMEM((HALO + tile, ch), F32),
                        pltpu.VMEM((SUBLANE - 1, HALO + tile - SUBLANE, ch), F32),
                        pltpu.VMEM((tile, ch), F32)],
        input_output_aliases=aliases,
        compiler_params=_params(1),
        name="conv_mixer",
    )(*args)


def _strict_upper_sum_matrix(n):
    r = lax.broadcasted_iota(jnp.int32, (n, n), 0)
    c = lax.broadcasted_iota(jnp.int32, (n, n), 1)
    return jnp.where(r > c, 1.0, 0.0).astype(BF16)


_SB_DEAD = -105.0


def _sb_tile(zn, v, u_mat, carry, mask):
    lse = jnp.log(1.0 + jnp.exp(-jnp.abs(zn)))
    log_rest = jnp.minimum(zn, 0.0) - lse
    if mask is not None:
        log_rest = jnp.where(mask, log_rest, 0.0)
    hi = log_rest.astype(BF16)
    lo = (log_rest - hi.astype(F32)).astype(BF16)
    excl = _dot(hi, u_mat) + _dot(lo, u_mat)
    logit = (log_rest - zn) + excl + _rep(carry, zn.shape[1])
    w = jnp.exp(logit)
    if mask is not None:
        w = jnp.where(mask, w, 0.0)
    out = _dot(w.astype(BF16), v)
    return out, carry + _row_reduce(log_rest, jnp.add, jnp.sum)


def _sb_chunk(qn, k_at, v_at, n_sub, ts, u_mat, carry, mask_at):
    acc = None
    for s in reversed(range(n_sub)):
        out, carry = _sb_tile(_dot_nt(qn, k_at(s)), v_at(s), u_mat, carry, mask_at(s))
        acc = out if acc is None else acc + out
    return acc, carry


def _sb_prompt_kernel(q_ref, k_ref, v_ref, g_ref, o_ref, acc_ref, carry_ref, *, scale, valid_from, ts):
    i = pl.program_id(1)
    ta = q_ref.shape[0]
    n_sub = ta // ts
    qn = (q_ref[...] * (-scale)).astype(BF16)
    u_mat = _strict_upper_sum_matrix(ts)
    row = lax.broadcasted_iota(jnp.int32, (ta, ts), 0)
    col = lax.broadcasted_iota(jnp.int32, (ta, ts), 1)

    def sweep(first_key, n_tiles, mask_at):
        k_at = lambda s: k_ref[pl.ds(pl.multiple_of(first_key + s * ts, ts), ts), :].astype(BF16)
        v_at = lambda s: v_ref[pl.ds(pl.multiple_of(first_key + s * ts, ts), ts), :].astype(BF16)
        out, carry = _sb_chunk(qn, k_at, v_at, n_tiles, ts, u_mat, carry_ref[...], mask_at)
        acc_ref[...] += out
        carry_ref[...] = carry

    acc_ref[...] = jnp.zeros_like(acc_ref)
    carry_ref[...] = jnp.zeros_like(carry_ref)
    sweep(i * ta, n_sub, lambda s: (col + s * ts < row) & (col + s * ts + i * ta >= valid_from))

    def live(c):
        n, top = c
        return (n < jnp.where(i >= 1, n_sub + i - 1, 0)) & (top > _SB_DEAD)

    def body(c):
        n = c[0]

        @pl.when(n < n_sub)
        def _():
            first_key = (i - 1) * ta + (n_sub - 1 - n) * ts
            sweep(first_key, 1, lambda s: col + first_key >= valid_from)

        @pl.when(n >= n_sub)
        def _():
            first_key = (i - 2 - (n - n_sub)) * ta
            sweep(first_key, n_sub, lambda s: (col + first_key >= valid_from) if s == 0 else None)
        return n + 1, jnp.max(carry_ref[...])
    lax.while_loop(live, body, (jnp.int32(0), jnp.max(carry_ref[...])))

    o_ref[...] = (acc_ref[...] * _silu(g_ref[...])).astype(o_ref.dtype)


def _sb_prompt(z, c_q, c_k, c_v, c_gate, n_heads, dh, *, rows, valid_from):
    mp = z.shape[0]
    assert rows % TQ == 0 and all(c % dh == 0 for c in (c_q, c_k, c_v, c_gate)) and dh == LANE
    ta = _pick_tile(rows, ATTN_CHUNK, TQ)
    bq, bk, bv, bg = (c // dh for c in (c_q, c_k, c_v, c_gate))
    return pl.pallas_call(
        functools.partial(_sb_prompt_kernel, scale=dh ** -0.5, valid_from=valid_from, ts=TQ),
        grid=(n_heads, rows // ta),
        in_specs=[pl.BlockSpec((ta, dh), lambda h, i: (i, bq + h)),
                  pl.BlockSpec((rows, dh), lambda h, i: (0, bk + h)),
                  pl.BlockSpec((rows, dh), lambda h, i: (0, bv + h)),
                  pl.BlockSpec((ta, dh), lambda h, i: (i, bg + h))],
        out_specs=pl.BlockSpec((ta, dh), lambda h, i: (i, h)),
        out_shape=jax.ShapeDtypeStruct((mp, n_heads * dh), BF16),
        scratch_shapes=[pltpu.VMEM((ta, dh), F32), pltpu.VMEM((ta, LANE), F32)],
        compiler_params=_params(2),
        name="sb_prompt",
    )(z, z, z, z)


def _sb_sample_kernel(q_ref, kn_ref, vn_ref, kc_ref, vc_ref, g_ref, yb_hbm, o_ref, *, scale, tk, n_sub, nh):
    del yb_hbm
    h = pl.program_id(1)
    tq = q_ref.shape[0]
    past = kc_ref.shape[1] // nh
    qn = (q_ref[...] * (-scale)).astype(BF16)
    tn = max(tq, LANE)
    zpad = jnp.zeros((tn - tq, kn_ref.shape[1]), F32)
    kn = jnp.concatenate([kn_ref[...], zpad], axis=0).astype(BF16)
    vn = jnp.concatenate([vn_ref[...], zpad], axis=0).astype(BF16)
    row = lax.broadcasted_iota(jnp.int32, (tq, tn), 0)
    col = lax.broadcasted_iota(jnp.int32, (tq, tn), 1)
    acc, carry = _sb_tile(_dot_nt(qn, kn), vn, _strict_upper_sum_matrix(tn),
                          jnp.zeros((tq, LANE), F32), col < row)
    u_mat = _strict_upper_sum_matrix(tk)
    ck = n_sub * tk

    def live(c):
        return (c[0] < past // ck) & (c[1] > _SB_DEAD)

    def body(c):
        n, _, acc, carry = c
        ks = past - (n + 1) * ck
        rows = lambda s: pl.ds((ks + s * tk) * nh + h, tk, stride=nh)
        k_at = lambda s: kc_ref[0, rows(s), :].astype(BF16)
        v_at = lambda s: vc_ref[0, rows(s), :].astype(BF16)
        out, carry = _sb_chunk(qn, k_at, v_at, n_sub, tk, u_mat, carry, lambda s: None)
        return n + 1, jnp.max(carry), acc + out, carry
    _, _, acc, carry = lax.while_loop(live, body, (jnp.int32(0), jnp.max(carry), acc, carry))
    o_ref[...] = (acc * _silu(g_ref[...])).astype(o_ref.dtype)


def _sb_sample(z, cache_k, cache_v, layer, yb, c_q, c_k, c_v, c_gate, *, row_off, t_new):
    depth, nb, past, n_heads, dh = cache_k.shape
    assert row_off % t_new == 0 and dh == LANE
    tk = _pick_tile(past, TQ, LANE)
    n_sub = _pick_tile(past // tk, 2, 1)
    rb = row_off // t_new
    bq, bk, bv, bg = (c // dh for c in (c_q, c_k, c_v, c_gate))
    row_spec = lambda cb: pl.BlockSpec((t_new, dh), lambda b, h: (rb + b, cb + h))
    cache_spec = pl.BlockSpec((None, 1, past * n_heads, dh), lambda b, h: (layer, b, 0, 0))
    cache_k = cache_k.reshape(depth, nb, past * n_heads, dh)
    cache_v = cache_v.reshape(depth, nb, past * n_heads, dh)
    return pl.pallas_call(
        functools.partial(_sb_sample_kernel, scale=dh ** -0.5, tk=tk, n_sub=n_sub, nh=n_heads),
        grid=(nb, n_heads),
        in_specs=[row_spec(bq), row_spec(bk), row_spec(bv), cache_spec, cache_spec, row_spec(bg),
                  pl.BlockSpec(memory_space=pl.ANY)],
        out_specs=pl.BlockSpec((t_new, dh), lambda b, h: (rb + b, h)),
        out_shape=jax.ShapeDtypeStruct(yb.shape, yb.dtype),
        input_output_aliases={6: 0},
        compiler_params=_params(2),
        name="sb_sample",
    )(z, z, z, cache_k, cache_v, z, yb)


def _q_up_kernel(cq_ref, w_ref, cos_ref, sin_ref, o_ref, *, scale, d_nope):
    cq = cq_ref[...]
    cos = cos_ref[...]
    sin = sin_ref[...]
    d_rope = cos.shape[1]
    for h in range(w_ref.shape[0]):
        r = _dot(cq, w_ref[h])
        rope = r[:, d_nope:d_nope + d_rope] * cos + r[:, d_nope + d_rope:] * sin
        o_ref[h, :, 0:d_nope] = (r[:, :d_nope] * scale).astype(o_ref.dtype)
        o_ref[h, :, d_nope:] = (rope * scale).astype(o_ref.dtype)


def _q_up(cq, wq, cos, sin, *, d_nope, scale):
    m, qr = cq.shape
    nh, _, wcols = wq.shape
    d_rope = cos.shape[1]
    assert wcols == d_nope + 2 * d_rope
    tm = _pick_tile(m, 640, 128)
    return pl.pallas_call(
        functools.partial(_q_up_kernel, scale=scale, d_nope=d_nope),
        grid=(m // tm,),
        in_specs=[pl.BlockSpec((tm, qr), lambda i: (i, 0)),
                  pl.BlockSpec((nh, qr, wcols), lambda i: (0, 0, 0)),
                  pl.BlockSpec((tm, d_rope), lambda i: (i, 0)),
                  pl.BlockSpec((tm, d_rope), lambda i: (i, 0))],
        out_specs=pl.BlockSpec((nh, tm, d_nope + d_rope), lambda i: (0, i, 0)),
        out_shape=jax.ShapeDtypeStruct((nh, m, d_nope + d_rope), BF16),
        compiler_params=_params(1),
        name="q_up",
    )(cq, wq, cos, sin)


def _kv_up_kernel(lat_ref, kr_ref, w_ref, k_ref, v_ref, *, d_nope):
    lat = lat_ref[...].astype(BF16)
    kr = kr_ref[...].astype(BF16)
    for h in range(w_ref.shape[0]):
        r = _dot(lat, w_ref[h])
        k_ref[h, :, 0:d_nope] = r[:, :d_nope].astype(BF16)
        k_ref[h, :, d_nope:] = kr
        v_ref[h] = r[:, d_nope:].astype(BF16)


def _kv_up(lat, kr, wkv, *, rows, d_nope):
    kvr = lat.shape[1]
    d_rope = kr.shape[1]
    nh, _, wcols = wkv.shape
    d_v = wcols - d_nope
    tr = _pick_tile(rows, 768, TQ)
    return pl.pallas_call(
        functools.partial(_kv_up_kernel, d_nope=d_nope),
        grid=(rows // tr,),
        in_specs=[pl.BlockSpec((tr, kvr), lambda i: (i, 0)),
                  pl.BlockSpec((tr, d_rope), lambda i: (i, 0)),
                  pl.BlockSpec((nh, kvr, wcols), lambda i: (0, 0, 0))],
        out_specs=[pl.BlockSpec((nh, tr, d_nope + d_rope), lambda i: (0, i, 0)),
                   pl.BlockSpec((nh, tr, d_v), lambda i: (0, i, 0))],
        out_shape=[jax.ShapeDtypeStruct((nh, rows, d_nope + d_rope), BF16),
                   jax.ShapeDtypeStruct((nh, rows, d_v), BF16)],
        compiler_params=_params(1),
        name="kv_up",
    )(lat, kr, wkv)


def _mla_prompt_kernel(q_ref, k_ref, v_ref, g_ref, o_ref, m_ref, l_ref, acc_ref, *, valid_from):
    i = pl.program_id(1)
    hg, ta, _ = q_ref.shape
    dv = v_ref.shape[2]

    def chunk(j, mask):
        ks = pl.multiple_of(j * ta, ta)
        for hh in range(hg):
            s = _dot_nt(q_ref[hh], k_ref[hh, pl.ds(ks, ta), :])
            if mask is not None:
                s = jnp.where(mask, s, NEG_INF)
            m_old = m_ref[hh]
            m_new = jnp.maximum(m_old, _row_reduce(s, jnp.maximum, jnp.max))
            alpha = jnp.exp2(m_old - m_new)
            p = jnp.exp2(s - _rep(m_new, ta))
            l_ref[hh] = alpha * l_ref[hh] + _row_reduce(p, jnp.add, jnp.sum)
            acc_ref[hh] = alpha * acc_ref[hh] + _dot(p.astype(BF16), v_ref[hh, pl.ds(ks, ta), :])
            m_ref[hh] = m_new

    m_ref[...] = jnp.full_like(m_ref, NEG_INF)
    l_ref[...] = jnp.zeros_like(l_ref)
    acc_ref[...] = jnp.zeros_like(acc_ref)
    col = lax.broadcasted_iota(jnp.int32, (ta, ta), 1)

    @pl.when(i >= 1)
    def _():
        chunk(0, col >= valid_from)

    def body(j, c):
        chunk(j, None)
        return c
    lax.fori_loop(1, i, body, 0)

    first = TQ // CHUNK - 1
    qc = jnp.maximum((i * ta + lax.broadcasted_iota(jnp.int32, (ta, ta), 0)) // CHUNK - first, 0)
    kc = jnp.maximum((i * ta + col) // CHUNK - first, 0)
    chunk(i, (kc <= qc) & (col + i * ta >= valid_from))

    for hh in range(hg):
        cols = slice(hh * dv, (hh + 1) * dv)
        o_ref[:, cols] = (acc_ref[hh] / l_ref[hh] * _silu(g_ref[:, cols])).astype(o_ref.dtype)


def _mla_prompt(q_cat, k_cat, v, z, c_gate, *, rows, valid_from):
    nh, mp, dk = q_cat.shape
    dv = v.shape[2]
    hg = next(g for g in (4, 2, 1) if nh % g == 0)
    assert rows % TQ == 0 and c_gate % (hg * dv) == 0 and dv == LANE and TQ % CHUNK == 0
    ta = _pick_tile(rows, ATTN_CHUNK, TQ)
    bg = c_gate // (hg * dv)
    return pl.pallas_call(
        functools.partial(_mla_prompt_kernel, valid_from=valid_from),
        grid=(nh // hg, rows // ta),
        in_specs=[pl.BlockSpec((hg, ta, dk), lambda h, i: (h, i, 0)),
                  pl.BlockSpec((hg, rows, dk), lambda h, i: (h, 0, 0), pipeline_mode=pl.Buffered(1)),
                  pl.BlockSpec((hg, rows, dv), lambda h, i: (h, 0, 0), pipeline_mode=pl.Buffered(1)),
                  pl.BlockSpec((ta, hg * dv), lambda h, i: (i, bg + h))],
        out_specs=pl.BlockSpec((ta, hg * dv), lambda h, i: (i, h)),
        out_shape=jax.ShapeDtypeStruct((mp, nh * dv), BF16),
        scratch_shapes=[pltpu.VMEM((hg, ta, LANE), F32), pltpu.VMEM((hg, ta, LANE), F32),
                        pltpu.VMEM((hg, ta, dv), F32)],
        compiler_params=_params(2),
        name="mla_prompt",
    )(q_cat, k_cat, v, z)


def _mla_sample_kernel(q_ref, latc_ref, krc_ref, latn_ref, krn_ref, w_ref, g_ref, yc_hbm, o_ref,
                       ql_ref, qr_ref, *, d_nope):
    del yc_hbm
    nh, t_new, _ = q_ref.shape
    latc = latc_ref[0].astype(BF16)
    krc = krc_ref[0].astype(BF16)
    latn = latn_ref[...].astype(BF16)
    krn = krn_ref[...].astype(BF16)
    for h in range(nh):
        qh = q_ref[h]
        ql_ref[h * t_new:(h + 1) * t_new, :] = _dot_nt(qh[:, :d_nope], w_ref[h, :, :d_nope]).astype(BF16)
        qr_ref[h * t_new:(h + 1) * t_new, :] = qh[:, d_nope:]
    ql = ql_ref[...]
    qr = qr_ref[...]
    s_c = _dot_nt(ql, latc) + _dot_nt(qr, krc)
    s_n = _dot_nt(ql, latn) + _dot_nt(qr, krn)
    m = jnp.maximum(jnp.max(s_c, axis=1, keepdims=True), jnp.max(s_n, axis=1, keepdims=True))
    p_c = jnp.exp2(s_c - m)
    p_n = jnp.exp2(s_n - m)
    denom = jnp.sum(p_c, axis=1, keepdims=True) + jnp.sum(p_n, axis=1, keepdims=True)
    o_lat = ((_dot(p_c.astype(BF16), latc) + _dot(p_n.astype(BF16), latn)) / denom).astype(BF16)
    d_v = w_ref.shape[2] - d_nope
    for h in range(nh):
        yh = _dot(o_lat[h * t_new:(h + 1) * t_new, :], w_ref[h, :, d_nope:])
        o_ref[:, h * d_v:(h + 1) * d_v] = (yh * _silu(g_ref[:, h * d_v:(h + 1) * d_v])).astype(o_ref.dtype)


def _mla_sample(q_cat, lat_cache, kr_cache, layer, lat_new, kr_new, wkv, z, yc, c_gate, *, row_off, t_new, d_nope):
    nh, mp, dk = q_cat.shape
    _, nb, past, kvr = lat_cache.shape
    d_rope = kr_cache.shape[3]
    d_v = wkv.shape[2] - d_nope
    width = nh * d_v
    assert row_off % t_new == 0 and c_gate % width == 0
    rb = row_off // t_new
    return pl.pallas_call(
        functools.partial(_mla_sample_kernel, d_nope=d_nope),
        grid=(nb,),
        in_specs=[pl.BlockSpec((nh, t_new, dk), lambda b: (0, rb + b, 0)),
                  pl.BlockSpec((None, 1, past, kvr), lambda b: (layer, b, 0, 0)),
                  pl.BlockSpec((None, 1, past, d_rope), lambda b: (layer, b, 0, 0)),
                  pl.BlockSpec((t_new, kvr), lambda b: (rb + b, 0)),
                  pl.BlockSpec((t_new, d_rope), lambda b: (rb + b, 0)),
                  pl.BlockSpec(wkv.shape, lambda b: (0, 0, 0)),
                  pl.BlockSpec((t_new, width), lambda b: (rb + b, c_gate // width)),
                  pl.BlockSpec(memory_space=pl.ANY)],
        out_specs=pl.BlockSpec((t_new, width), lambda b: (rb + b, 0)),
        out_shape=jax.ShapeDtypeStruct(yc.shape, yc.dtype),
        scratch_shapes=[pltpu.VMEM((nh * t_new, kvr), BF16), pltpu.VMEM((nh * t_new, d_rope), BF16)],
        input_output_aliases={7: 0},
        compiler_params=_params(1),
        name="mla_sample",
    )(q_cat, lat_cache, kr_cache, lat_new, kr_new, wkv, z, yc)


def _out_proj_kernel(ya_ref, yb_ref, yc_ref, wa_ref, wb_ref, wc_ref, h_ref, o_ref, *, rc):
    def body(r, carry):
        rs = pl.multiple_of(r * rc, rc)
        sl = pl.ds(rs, rc)
        y = _dot(ya_ref[sl, :], wa_ref[...]) + _dot(yb_ref[sl, :], wb_ref[...]) + _dot(yc_ref[sl, :], wc_ref[...])
        o_ref[sl, :] = h_ref[sl, :] + y
        return carry
    lax.fori_loop(0, ya_ref.shape[0] // rc, body, 0)


def _out_proj(ya, yb, yc, w_out, h):
    m, d = h.shape
    da, db, dc = ya.shape[1], yb.shape[1], yc.shape[1]
    assert da == db and dc % da == 0 and w_out.shape == (da + db + dc, d)
    tm = _pick_tile(m, 1280, 128)
    tn = _pick_tile(d, 512, 128)
    rc = _pick_tile(tm, 640, 128)
    return pl.pallas_call(
        functools.partial(_out_proj_kernel, rc=rc),
        grid=(m // tm, d // tn),
        in_specs=[pl.BlockSpec((tm, da), lambda i, j: (i, 0)),
                  pl.BlockSpec((tm, db), lambda i, j: (i, 0)),
                  pl.BlockSpec((tm, dc), lambda i, j: (i, 0)),
                  pl.BlockSpec((da, tn), lambda i, j: (0, j)),
                  pl.BlockSpec((db, tn), lambda i, j: (1, j)),
                  pl.BlockSpec((dc, tn), lambda i, j: ((da + db) // dc, j)),
                  pl.BlockSpec((tm, tn), lambda i, j: (i, j))],
        out_specs=pl.BlockSpec((tm, tn), lambda i, j: (i, j)),
        out_shape=jax.ShapeDtypeStruct((m, d), F32),
        compiler_params=_params(2),
        name="out_proj",
    )(ya, yb, yc, w_out, w_out, w_out, h)


def _swap_neg_halves(w):
    half = w.shape[-1] // 2
    return jnp.concatenate([-w[..., half:], w[..., :half]], axis=-1)


def kernel(x_prompt, x_sample, cache_a_conv, cache_b_k, cache_b_v, cache_c_latent, cache_c_krope, meta_tokens, norm_g, w_in, a_dw_w, a_dw_b, a_ln_g, a_ln_b, c_q_norm_g, c_w_uq, c_kv_norm_g, c_w_uk, c_w_uv, w_out, final_norm_g):
    bp, seq, d = x_prompt.shape
    nb, t_new, _ = x_sample.shape
    depth = w_in.shape[0]
    n_meta = meta_tokens.shape[0]
    hist, d_a = cache_a_conv.shape[2], cache_a_conv.shape[3]
    past, h_b, dh_b = cache_b_k.shape[2], cache_b_k.shape[3], cache_b_k.shape[4]
    d_b = h_b * dh_b
    kvr = cache_c_latent.shape[3]
    d_rope = cache_c_krope.shape[3]
    qr = c_q_norm_g.shape[1]
    h_c, d_nope = c_w_uk.shape[2], c_w_uk.shape[3]
    d_v = c_w_uv.shape[3]
    d_c = h_c * d_v
    assert bp == 1 and n_meta <= TQ and n_meta % 8 == 0 and seq % TQ == 0
    assert t_new == CHUNK and past % CHUNK == 0 and hist == a_dw_w.shape[1] - 1 and hist <= t_new
    assert d_nope == LANE and d_v == LANE and dh_b == LANE

    pad = TQ - n_meta
    p_end = TQ + seq
    s0 = p_end
    mp = s0 + nb * t_new
    assert (nb * t_new) % LANE == 0

    o_in = {}
    acc = 0
    for name, width in (("a_val", d_a), ("a_glu", d_a), ("a_gate", d_a), ("b_q", d_b), ("b_k", d_b),
                        ("b_v", d_b), ("b_gate", d_b), ("c_q", qr), ("c_kv", kvr), ("c_kr", d_rope),
                        ("c_gate", d_c)):
        o_in[name] = (acc, width)
        acc += width
    assert acc == w_in.shape[2]
    order = ("c_q", "c_kv", "c_gate", "a_val", "a_glu", "a_gate", "b_q", "b_k", "b_v", "b_gate")
    col = {}
    acc = 0
    for name in order:
        col[name] = acc
        acc += o_in[name][1]
    n_main = acc
    assert col["c_q"] % qr == 0 and col["c_kv"] % kvr == 0

    segments = tuple((o_in[n][0], col[n], o_in[n][1]) for n in order)
    w_main, w_kr2 = _w_prep(jnp.swapaxes(w_in, 1, 2), segments, o_in["c_kr"][0], d_rope, n_main)

    def layer_weights(l):
        wq4 = c_w_uq[l].reshape(qr, h_c, d_nope + d_rope)
        wq_rope = wq4[..., d_nope:]
        wq = jnp.concatenate([wq4[..., :d_nope], wq_rope, _swap_neg_halves(wq_rope)], axis=-1)
        wq = jnp.transpose(wq, (1, 0, 2)).astype(BF16)
        wkv = jnp.concatenate([jnp.transpose(c_w_uk[l], (1, 0, 2)), jnp.transpose(c_w_uv[l], (1, 0, 2))],
                              axis=-1).astype(BF16)
        return wq, wkv, w_out[l].astype(BF16)

    pos_prompt = jnp.maximum(jnp.arange(p_end) - pad, 0)
    pos_sample = past + (jnp.arange(nb * t_new) % t_new)
    pos = jnp.concatenate([pos_prompt, pos_sample]).astype(F32)
    half = d_rope // 2
    inv_freq = ROPE_BASE ** (-jnp.arange(half, dtype=F32) / half)
    ang = pos[:, None] * inv_freq[None, :]
    cos = jnp.tile(jnp.cos(ang), (1, 2))
    sin = jnp.tile(jnp.sin(ang), (1, 2))

    h = jnp.concatenate([jnp.zeros((pad, d), F32), meta_tokens.astype(F32), x_prompt[0],
                         x_sample.reshape(nb * t_new, d)], axis=0)
    conv_halo = jnp.pad(cache_a_conv, ((0, 0), (0, 0), (HALO - hist, 0), (0, 0)))
    mla_scale = (d_nope + d_rope) ** -0.5 * LOG2E

    outs = {k: [] for k in ("p_conv", "p_bk", "p_bv", "p_lat", "p_kr", "s_conv", "s_bk", "s_bv", "s_lat", "s_kr")}
    for l in range(depth):
        wq, wkv, w_out_b = layer_weights(l)
        xn = _rmsnorm(h, norm_g[l], BF16, rows=mp)
        z = _matmul(xn, w_main, l, F32)
        kr_new = _kr_proj(xn, w_kr2, l, cos, sin)

        cols_a = (col["a_val"], col["a_glu"], col["a_gate"])
        ya, u = _conv_mixer(z, cols_a, None, a_dw_w[l], a_dw_b[l], a_ln_g[l], a_ln_b[l],
                            tile=TQ, row_off=0, rows=p_end, valid_from=pad)
        ya, u = _conv_mixer(z, cols_a, conv_halo[l], a_dw_w[l], a_dw_b[l], a_ln_g[l], a_ln_b[l],
                            tile=t_new, row_off=s0, rows=nb * t_new, valid_from=0, prev=(ya, u))

        yb = _sb_prompt(z, col["b_q"], col["b_k"], col["b_v"], col["b_gate"], h_b, dh_b,
                        rows=p_end, valid_from=pad)
        yb = _sb_sample(z, cache_b_k, cache_b_v, l, yb, col["b_q"], col["b_k"], col["b_v"],
                        col["b_gate"], row_off=s0, t_new=t_new)

        cq = _rmsnorm(z, c_q_norm_g[l], BF16, rows=mp, col_block=col["c_q"] // qr)
        lat_new = _rmsnorm(z, c_kv_norm_g[l], F32, rows=mp, col_block=col["c_kv"] // kvr)
        q_cat = _q_up(cq, wq, cos, sin, d_nope=d_nope, scale=mla_scale)
        k_cat, v_up = _kv_up(lat_new, kr_new, wkv, rows=p_end, d_nope=d_nope)
        yc = _mla_prompt(q_cat, k_cat, v_up, z, col["c_gate"], rows=p_end, valid_from=pad)
        yc = _mla_sample(q_cat, cache_c_latent, cache_c_krope, l, lat_new, kr_new, wkv, z, yc,
                         col["c_gate"], row_off=s0, t_new=t_new, d_nope=d_nope)

        h = _out_proj(ya, yb, yc, w_out_b, h)

        k_new = z[:, col["b_k"]:col["b_k"] + d_b]
        v_new = z[:, col["b_v"]:col["b_v"] + d_b]
        outs["p_conv"].append(u[p_end - hist:p_end].reshape(1, hist, d_a))
        outs["p_bk"].append(k_new[pad:p_end].reshape(1, n_meta + seq, h_b, dh_b))
        outs["p_bv"].append(v_new[pad:p_end].reshape(1, n_meta + seq, h_b, dh_b))
        outs["p_lat"].append(lat_new[pad:p_end].reshape(1, n_meta + seq, kvr))
        outs["p_kr"].append(kr_new[pad:p_end].reshape(1, n_meta + seq, d_rope))
        outs["s_conv"].append(u[s0:].reshape(nb, t_new, d_a)[:, t_new - hist:])
        outs["s_bk"].append(k_new[s0:].reshape(nb, t_new, h_b, dh_b))
        outs["s_bv"].append(v_new[s0:].reshape(nb, t_new, h_b, dh_b))
        outs["s_lat"].append(lat_new[s0:].reshape(nb, t_new, kvr))
        outs["s_kr"].append(kr_new[s0:].reshape(nb, t_new, d_rope))

    y_prompt = _rmsnorm(h, final_norm_g, F32, rows=seq, row_off=TQ).reshape(1, seq, d)
    y_sample = _rmsnorm(h, final_norm_g, F32, rows=nb * t_new, row_off=s0, tr=t_new).reshape(nb, t_new, d)
    st = lambda k: jnp.stack(outs[k])
    return (y_prompt, y_sample, st("p_conv"), st("p_bk"), st("p_bv"), st("p_lat"), st("p_kr"),
            st("s_conv"), st("s_bk"), st("s_bv"), st("s_lat"), st("s_kr"))
```
